```python
import math
import jax, jax.numpy as jnp
from jax import lax
import numpy as np

D_MODEL = 1024
BATCH = 8
SEQ = 8192
DEPTH = 2

NUM_META = 16
MIX_WIDTH = D_MODEL
ATTN_HEADS = D_MODEL // 256
QK_NOPE_DIM = 128
QK_ROPE_DIM = 64
V_HEAD_DIM = 128
Q_LORA_RANK = 3 * D_MODEL // 8
KV_LORA_RANK = D_MODEL // 4
ATTN_WIDTH = ATTN_HEADS * V_HEAD_DIM
ATTN_SCALE = 1.0 / math.sqrt(QK_NOPE_DIM + QK_ROPE_DIM)
ROPE_THETA = 10000.0
Q_BLOCK = 128
SSM_WIDTH = MIX_WIDTH - ATTN_WIDTH
SSM_GROUP = 16
SSM_GROUPS = SSM_WIDTH // SSM_GROUP
SSM_STATE = 64
SSM_CHUNK = 128
DT_MIN = 0.001
DT_MAX = 0.1
IN_WIDTH = Q_LORA_RANK + KV_LORA_RANK + QK_ROPE_DIM + SSM_WIDTH
FFN_HIDDEN = -(-8 * D_MODEL // (3 * 256)) * 256
RMS_EPS = 1e-6

kernel_name = "hymba_mla_s5_hybrid_block"


def _rms_norm(x, g, eps=RMS_EPS):
    xf = x.astype(jnp.float32)
    y = xf * lax.rsqrt(jnp.mean(xf * xf, axis=-1, keepdims=True) + eps)
    return (y * g.astype(jnp.float32)).astype(x.dtype)


def _rope(x, cos, sin):
    half = x.shape[-1] // 2
    x1, x2 = x[..., :half], x[..., half:]
    return jnp.concatenate([x1 * cos - x2 * sin, x2 * cos + x1 * sin], axis=-1).astype(x.dtype)


def _attend(q_nope, q_rope, k_nope, k_rope, v, q_pos):
    s = (jnp.einsum('bthd,bshd->bhts', q_nope, k_nope)
         + jnp.einsum('bthr,bsr->bhts', q_rope, k_rope)).astype(jnp.float32) * ATTN_SCALE
    k_pos = jnp.arange(k_nope.shape[1])
    s = jnp.where(k_pos[None, :] <= q_pos[:, None], s, -jnp.inf)
    p = jax.nn.softmax(s, axis=-1).astype(v.dtype)
    return jnp.einsum('bhts,bshd->bthd', p, v)


def _mla_mixer(c_q, c_kv, k_rope_in, q_norm_g, w_uq, kv_norm_g, w_ukv, cos, sin):
    b, L, _ = c_q.shape
    M = NUM_META
    q = (_rms_norm(c_q, q_norm_g) @ w_uq).reshape(b, L, ATTN_HEADS, QK_NOPE_DIM + QK_ROPE_DIM)
    q_nope = q[..., :QK_NOPE_DIM]
    q_rope = _rope(q[..., QK_NOPE_DIM:], cos[:, None], sin[:, None])
    kv = (_rms_norm(c_kv, kv_norm_g) @ w_ukv).reshape(b, L, ATTN_HEADS, QK_NOPE_DIM + V_HEAD_DIM)
    k_nope, v = kv[..., :QK_NOPE_DIM], kv[..., QK_NOPE_DIM:]
    k_rope = _rope(k_rope_in, cos, sin)
    pos = jnp.arange(L)
    out_meta = _attend(q_nope[:, :M], q_rope[:, :M], k_nope[:, :M], k_rope[:, :M], v[:, :M], pos[:M])
    n = (L - M) // Q_BLOCK

    def blocks(t):
        return jnp.moveaxis(t[:, M:].reshape(b, n, Q_BLOCK, *t.shape[2:]), 1, 0)

    out_real = lax.map(lambda a: _attend(a[0], a[1], k_nope, k_rope, v, a[2]),
                       (blocks(q_nope), blocks(q_rope), pos[M:].reshape(n, Q_BLOCK)))
    out_real = jnp.moveaxis(out_real, 0, 1).reshape(b, L - M, ATTN_HEADS, V_HEAD_DIM)
    out = jnp.concatenate([out_meta, out_real], axis=1)
    return out.reshape(b, L, ATTN_WIDTH)


def _cplx_linear_op(e1, e2):
    a1r, a1i, b1r, b1i = e1
    a2r, a2i, b2r, b2i = e2
    ar = a2r * a1r - a2i * a1i
    ai = a2r * a1i + a2i * a1r
    br = a2r * b1r - a2i * b1i + b2r
    bi = a2r * b1i + a2i * b1r + b2i
    return ar, ai, br, bi


def _ssm_mixer(u, a_re, a_im, log_dt, b_re, b_im, c_re, c_im, d_skip, w_glu):
    f32 = jnp.float32
    b, L, _ = u.shape
    M = NUM_META
    lr = jnp.minimum(a_re.astype(f32), -1e-4)
    li = a_im.astype(f32)
    dt = jnp.exp(log_dt.astype(f32))[:, None]
    mag = jnp.exp(lr * dt)
    lam_re = mag * jnp.cos(li * dt)
    lam_im = mag * jnp.sin(li * dt)
    nr, ni = lam_re - 1.0, lam_im
    den = lr * lr + li * li
    coef_re = ((nr * lr + ni * li) / den)[..., None]
    coef_im = ((ni * lr - nr * li) / den)[..., None]
    br, bi = b_re.astype(f32), b_im.astype(f32)
    bbar_re = coef_re * br - coef_im * bi
    bbar_im = coef_re * bi + coef_im * br
    cr_w, ci_w = c_re.astype(f32), c_im.astype(f32)

    def chunk(carry, u_c):
        sr0, si0 = carry
        bu_re = jnp.einsum('btgh,gph->btgp', u_c, bbar_re)
        bu_im = jnp.einsum('btgh,gph->btgp', u_c, bbar_im)
        a_r = jnp.broadcast_to(lam_re, bu_re.shape)
        a_i = jnp.broadcast_to(lam_im, bu_im.shape)
        acr, aci, xr, xi = lax.associative_scan(_cplx_linear_op, (a_r, a_i, bu_re, bu_im), axis=1)
        sr = xr + acr * sr0[:, None] - aci * si0[:, None]
        si = xi + acr * si0[:, None] + aci * sr0[:, None]
        y = jnp.einsum('btgp,ghp->btgh', sr, cr_w) - jnp.einsum('btgp,ghp->btgh', si, ci_w)
        return (sr[:, -1], si[:, -1]), y

    uf = u.astype(f32).reshape(b, L, SSM_GROUPS, SSM_GROUP)
    zeros = jnp.zeros((b, SSM_GROUPS, SSM_STATE), f32)
    carry, y_meta = chunk((zeros, zeros), uf[:, :M])
    n = (L - M) // SSM_CHUNK
    u_chunks = jnp.moveaxis(uf[:, M:].reshape(b, n, SSM_CHUNK, SSM_GROUPS, SSM_GROUP), 1, 0)
    _, y_real = lax.scan(chunk, carry, u_chunks)
    y_real = jnp.moveaxis(y_real, 0, 1).reshape(b, L - M, SSM_WIDTH)
    y = jnp.concatenate([y_meta.reshape(b, M, SSM_WIDTH), y_real], axis=1) + d_skip.astype(f32) * uf.reshape(b, L, SSM_WIDTH)
    g = jax.nn.gelu(y).astype(u.dtype)
    return g * jax.nn.sigmoid(g @ w_glu)


def setup_inputs(seed: int = 0) -> dict:
    key = jax.random.key(seed)
    ks = jax.random.split(key, 32)
    f32 = jnp.float32

    def nrm(k, shape, scale):
        return jax.random.normal(k, shape, f32) * scale

    def gain(k, shape):
        return 1.0 + 0.02 * jax.random.normal(k, shape, f32)

    G, P, HC = SSM_GROUPS, SSM_STATE, SSM_GROUP
    n_idx = jnp.arange(P, dtype=f32)
    return {
        "x": nrm(ks[0], (BATCH, SEQ, D_MODEL), 1.0),
        "meta_tokens": nrm(ks[1], (NUM_META, D_MODEL), 1.0),
        "norm_mix_g": gain(ks[2], (DEPTH, D_MODEL)),
        "w_in": nrm(ks[3], (DEPTH, D_MODEL, IN_WIDTH), D_MODEL ** -0.5),
        "q_norm_g": gain(ks[4], (DEPTH, Q_LORA_RANK)),
        "w_uq": nrm(ks[5], (DEPTH, Q_LORA_RANK, ATTN_HEADS * (QK_NOPE_DIM + QK_ROPE_DIM)), Q_LORA_RANK ** -0.5),
        "kv_norm_g": gain(ks[6], (DEPTH, KV_LORA_RANK)),
        "w_ukv": nrm(ks[7], (DEPTH, KV_LORA_RANK, ATTN_HEADS * (QK_NOPE_DIM + V_HEAD_DIM)), KV_LORA_RANK ** -0.5),
        "ssm_a_re": -0.5 + 0.01 * jax.random.normal(ks[8], (DEPTH, G, P), f32),
        "ssm_a_im": math.pi * n_idx + 0.01 * jax.random.normal(ks[9], (DEPTH, G, P), f32),
        "ssm_log_dt": jax.random.uniform(ks[10], (DEPTH, G), f32, math.log(DT_MIN), math.log(DT_MAX)),
        "ssm_b_re": nrm(ks[11], (DEPTH, G, P, HC), (2 * HC) ** -0.5),
        "ssm_b_im": nrm(ks[12], (DEPTH, G, P, HC), (2 * HC) ** -0.5),
        "ssm_c_re": nrm(ks[13], (DEPTH, G, HC, P), P ** -0.5),
        "ssm_c_im": nrm(ks[14], (DEPTH, G, HC, P), P ** -0.5),
        "ssm_d": nrm(ks[15], (DEPTH, SSM_WIDTH), 1.0),
        "w_glu": nrm(ks[16], (DEPTH, SSM_WIDTH, SSM_WIDTH), SSM_WIDTH ** -0.5),
        "attn_out_g": gain(ks[17], (DEPTH, ATTN_WIDTH)),
        "ssm_out_g": gain(ks[18], (DEPTH, SSM_WIDTH)),
        "w_o": nrm(ks[19], (DEPTH, MIX_WIDTH, D_MODEL), MIX_WIDTH ** -0.5),
        "norm_ffn_g": gain(ks[20], (DEPTH, D_MODEL)),
        "w_gate": nrm(ks[21], (DEPTH, D_MODEL, FFN_HIDDEN), D_MODEL ** -0.5),
        "w_up": nrm(ks[22], (DEPTH, D_MODEL, FFN_HIDDEN), D_MODEL ** -0.5),
        "w_down": nrm(ks[23], (DEPTH, FFN_HIDDEN, D_MODEL), FFN_HIDDEN ** -0.5),
        "final_norm_g": gain(ks[24], (D_MODEL,)),
    }


def reference(x, meta_tokens, norm_mix_g, w_in, q_norm_g, w_uq, kv_norm_g, w_ukv,
              ssm_a_re, ssm_a_im, ssm_log_dt, ssm_b_re, ssm_b_im, ssm_c_re, ssm_c_im,
              ssm_d, w_glu, attn_out_g, ssm_out_g, w_o, norm_ffn_g, w_gate, w_up, w_down,
              final_norm_g):
    b = x.shape[0]
    M = NUM_META
    h = jnp.concatenate([jnp.broadcast_to(meta_tokens[None].astype(x.dtype), (b, M, D_MODEL)), x], axis=1)
    L = h.shape[1]
    pos_f = jnp.arange(L, dtype=jnp.float32)
    inv_freq = 1.0 / (ROPE_THETA ** (jnp.arange(0, QK_ROPE_DIM, 2, dtype=jnp.float32) / QK_ROPE_DIM))
    ang = pos_f[:, None] * inv_freq[None, :]
    cos, sin = jnp.cos(ang), jnp.sin(ang)
    splits = [Q_LORA_RANK, Q_LORA_RANK + KV_LORA_RANK, Q_LORA_RANK + KV_LORA_RANK + QK_ROPE_DIM]
    for l in range(DEPTH):
        hn = _rms_norm(h, norm_mix_g[l])
        z = hn @ w_in[l]
        c_q, c_kv, k_r, u = jnp.split(z, splits, axis=-1)
        attn = _mla_mixer(c_q, c_kv, k_r, q_norm_g[l], w_uq[l], kv_norm_g[l], w_ukv[l], cos, sin)
        ssm = _ssm_mixer(u, ssm_a_re[l], ssm_a_im[l], ssm_log_dt[l], ssm_b_re[l], ssm_b_im[l],
                         ssm_c_re[l], ssm_c_im[l], ssm_d[l], w_glu[l])
        mixed = jnp.concatenate([_rms_norm(attn, attn_out_g[l]), _rms_norm(ssm, ssm_out_g[l])], axis=-1)
        h = h + mixed @ w_o[l]
        hn = _rms_norm(h, norm_ffn_g[l])
        h = h + (jax.nn.silu(hn @ w_gate[l]) * (hn @ w_up[l])) @ w_down[l]
    return _rms_norm(h, final_norm_g)[:, M:]
```

```python
import functools
import math

import jax
import jax.numpy as jnp
from jax import lax
from jax.experimental import pallas as pl
from jax.experimental.pallas import tpu as pltpu

F32 = jnp.float32
BF16 = jnp.bfloat16

D_MODEL = 1024
NUM_META = 16
ATTN_HEADS = 4
QK_NOPE_DIM = 128
QK_ROPE_DIM = 64
V_HEAD_DIM = 128
Q_LORA_RANK = 384
KV_LORA_RANK = 256
ATTN_WIDTH = ATTN_HEADS * V_HEAD_DIM
ATTN_SCALE = 1.0 / math.sqrt(QK_NOPE_DIM + QK_ROPE_DIM)
ROPE_THETA = 10000.0
SSM_WIDTH = 512
SSM_GROUP = 16
SSM_GROUPS = SSM_WIDTH // SSM_GROUP
SSM_STATE = 64
FFN_HIDDEN = 2816
RMS_EPS = 1e-6

LANE = 128
HEAD_PAD = 2 * LANE
SUB = 16
PAIR_W = 2 * SUB * SSM_GROUP
SEQ_TILE = 640
FFN_CHUNK = 1408
NEG_BIG = -1e30
VMEM_LIMIT = 56 * 1024 * 1024


def _rms(x, g):
    y = x * lax.rsqrt(jnp.mean(x * x, axis=-1, keepdims=True) + RMS_EPS)
    return y * g


def _proj_kernel(h_ref, ct_ref, st_ref, gmix_ref, win_ref, gq_ref, wuq_ref, gkv_ref, wukv_ref,
                 q_ref, k_ref, v_ref, u_ref):
    hn = _rms(h_ref[...], gmix_ref[...]).astype(BF16)
    z = jnp.dot(hn, win_ref[...], preferred_element_type=F32)
    cq = _rms(z[:, :Q_LORA_RANK], gq_ref[...]).astype(BF16)
    c0 = Q_LORA_RANK + KV_LORA_RANK
    ckv = _rms(z[:, Q_LORA_RANK:c0], gkv_ref[...]).astype(BF16)
    u_ref[...] = z[:, c0:c0 + SSM_WIDTH].astype(u_ref.dtype)
    ct = ct_ref[...]
    st = st_ref[...]

    def rope(a):
        return a * ct + pltpu.roll(a, 2 * (QK_ROPE_DIM // 2), 1) * st

    kr = rope(z[:, c0 + SSM_WIDTH:]).astype(BF16)
    q = jnp.dot(cq, wuq_ref[...], preferred_element_type=F32) * ATTN_SCALE
    kv = jnp.dot(ckv, wukv_ref[...], preferred_element_type=F32)
    for h in range(ATTN_HEADS):
        lo = HEAD_PAD * h
        q_ref[:, lo:lo + LANE] = q[:, lo:lo + LANE].astype(BF16)
        q_ref[:, lo + LANE:lo + HEAD_PAD] = rope(q[:, lo + LANE:lo + HEAD_PAD]).astype(BF16)
        k_ref[:, lo:lo + LANE] = kv[:, LANE * h:LANE * (h + 1)].astype(BF16)
        k_ref[:, lo + LANE:lo + HEAD_PAD] = kr
    v_ref[...] = kv[:, ATTN_WIDTH:].astype(BF16)


def _proj(h, ctab, stab, gmix, win, gq, wuq, gkv, wukv, *, tiles_per_seq):
    n = h.shape[0]
    tm = SEQ_TILE
    row = lambda i: (i, 0)
    pos = lambda i: (i % tiles_per_seq, 0)
    const = lambda i: (0, 0)
    full = lambda a: pl.BlockSpec(a.shape, const)
    return pl.pallas_call(
        _proj_kernel,
        grid=(n // tm,),
        in_specs=[pl.BlockSpec((tm, D_MODEL), row),
                  pl.BlockSpec((tm, LANE), pos), pl.BlockSpec((tm, LANE), pos),
                  full(gmix), full(win), full(gq), full(wuq), full(gkv), full(wukv)],
        out_specs=[pl.BlockSpec((tm, ATTN_HEADS * HEAD_PAD), row),
                   pl.BlockSpec((tm, ATTN_HEADS * HEAD_PAD), row),
                   pl.BlockSpec((tm, ATTN_WIDTH), row),
                   pl.BlockSpec((tm, SSM_WIDTH), row)],
        out_shape=[jax.ShapeDtypeStruct((n, ATTN_HEADS * HEAD_PAD), BF16),
                   jax.ShapeDtypeStruct((n, ATTN_HEADS * HEAD_PAD), BF16),
                   jax.ShapeDtypeStruct((n, ATTN_WIDTH), BF16),
                   jax.ShapeDtypeStruct((n, SSM_WIDTH), BF16)],
        compiler_params=pltpu.CompilerParams(dimension_semantics=("arbitrary",),
                                             vmem_limit_bytes=VMEM_LIMIT),
        name="proj",
    )(h, ctab, stab, gmix, win, gq, wuq, gkv, wukv)


def _attn_kernel(q_ref, k_ref, v_ref, o_ref, m_sc, l_sc, acc_sc, *, tile):
    i = pl.program_id(2)
    q = q_ref[0]
    m_sc[...] = jnp.full(m_sc.shape, NEG_BIG, F32)
    l_sc[...] = jnp.zeros(l_sc.shape, F32)
    acc_sc[...] = jnp.zeros(acc_sc.shape, F32)

    def step(start, diagonal):
        k = k_ref[0, pl.ds(start, tile), :]
        v = v_ref[0, pl.ds(start, tile), :]
        s = lax.dot_general(q, k, (((1,), (1,)), ((), ())), preferred_element_type=F32)
        if diagonal:
            rows = lax.broadcasted_iota(jnp.int32, s.shape, 0)
            cols = lax.broadcasted_iota(jnp.int32, s.shape, 1)
            s = jnp.where(cols <= rows, s, NEG_BIG)
        m_prev = m_sc[...]
        m_new = jnp.maximum(m_prev, jnp.max(s, axis=-1, keepdims=True))
        alpha = jnp.exp(m_prev - m_new)
        p = jnp.exp(s - m_new)
        l_sc[...] = alpha * l_sc[...] + jnp.sum(p, axis=-1, keepdims=True)
        acc_sc[...] = alpha * acc_sc[...] + jnp.dot(p.astype(BF16), v, preferred_element_type=F32)
        m_sc[...] = m_new

    def body(j, carry):
        step(pl.multiple_of(j * tile, tile), False)
        return carry

    lax.fori_loop(0, i, body, 0)
    step(pl.multiple_of(i * tile, tile), True)
    o_ref[0] = (acc_sc[...] / l_sc[...]).astype(o_ref.dtype)


def _attention(q, k, v):
    b, lp, _ = q.shape
    tile = SEQ_TILE
    return pl.pallas_call(
        functools.partial(_attn_kernel, tile=tile),
        grid=(b, ATTN_HEADS, lp // tile),
        in_specs=[pl.BlockSpec((1, tile, HEAD_PAD), lambda bi, hi, qi: (bi, qi, hi)),
                  pl.BlockSpec((1, lp, HEAD_PAD), lambda bi, hi, qi: (bi, 0, hi)),
                  pl.BlockSpec((1, lp, V_HEAD_DIM), lambda bi, hi, qi: (bi, 0, hi))],
        out_specs=pl.BlockSpec((1, tile, V_HEAD_DIM), lambda bi, hi, qi: (bi, qi, hi)),
        out_shape=jax.ShapeDtypeStruct((b, lp, ATTN_WIDTH), BF16),
        scratch_shapes=[pltpu.VMEM((tile, 1), F32), pltpu.VMEM((tile, 1), F32),
                        pltpu.VMEM((tile, V_HEAD_DIM), F32)],
        compiler_params=pltpu.CompilerParams(
            dimension_semantics=("arbitrary", "arbitrary", "arbitrary"),
            vmem_limit_bytes=VMEM_LIMIT),
        name="attention",
    )(q, k, v)


def _ssm_kernel(u_ref, ws_ref, a_ref, t0_ref, t1_ref, wc_ref, d_ref, o_ref, v_sc, sin_sc,
                *, batch):
    u = u_ref[0]
    half = PAIR_W // 2
    v_sc[...] = jnp.dot(u, ws_ref[0], preferred_element_type=F32)
    a_re = jnp.broadcast_to(a_ref[0, 0:1, :], (batch, LANE))
    a_im = jnp.broadcast_to(a_ref[0, 1:2, :], (batch, LANE))
    steps = u.shape[0] // batch

    def body(r, carry):
        s_re, s_im = carry
        rows = pl.ds(pl.multiple_of(r * batch, batch), batch)
        sin_sc[rows, 0:LANE] = s_re
        sin_sc[rows, LANE:2 * LANE] = s_im
        v_re = v_sc[rows, 0:LANE]
        v_im = v_sc[rows, LANE:2 * LANE]
        return (a_re * s_re - a_im * s_im + v_re, a_re * s_im + a_im * s_re + v_im)

    zero = jnp.zeros((batch, LANE), F32)
    lax.fori_loop(0, steps, body, (zero, zero), unroll=8)
    s_in = sin_sc[...].astype(BF16)
    y0 = jnp.dot(u[:, :half], t0_ref[0], preferred_element_type=F32)
    y1 = jnp.dot(u[:, half:], t1_ref[0], preferred_element_type=F32)
    yc = jnp.dot(s_in, wc_ref[0], preferred_element_type=F32)
    d = d_ref[0]
    o_ref[0, :, :half] = jax.nn.gelu(y0 + yc[:, :half] + d[:, :half] * u[:, :half].astype(F32)).astype(o_ref.dtype)
    o_ref[0, :, half:] = jax.nn.gelu(y1 + yc[:, half:] + d[:, half:] * u[:, half:].astype(F32)).astype(o_ref.dtype)


def _ssm(u_sub, ws, a16, toep, wc, d_sub, *, batch):
    pairs, rows, _ = u_sub.shape
    blk = lambda shape: pl.BlockSpec((1,) + shape, lambda p: (p, 0, 0))
    return pl.pallas_call(
        functools.partial(_ssm_kernel, batch=batch),
        grid=(pairs,),
        in_specs=[blk((rows, PAIR_W)), blk((PAIR_W, 2 * LANE)), blk((2, LANE)),
                  pl.BlockSpec((1, PAIR_W // 2, PAIR_W // 2), lambda p: (2 * p, 0, 0)),
                  pl.BlockSpec((1, PAIR_W // 2, PAIR_W // 2), lambda p: (2 * p + 1, 0, 0)),
                  blk((2 * LANE, PAIR_W)), blk((1, PAIR_W))],
        out_specs=blk((rows, PAIR_W)),
        out_shape=jax.ShapeDtypeStruct((pairs, rows, PAIR_W), BF16),
        scratch_shapes=[pltpu.VMEM((rows, 2 * LANE), F32), pltpu.VMEM((rows, 2 * LANE), F32)],
        compiler_params=pltpu.CompilerParams(dimension_semantics=("arbitrary",),
                                             vmem_limit_bytes=VMEM_LIMIT),
        name="ssm",
    )(u_sub, ws, a16, toep, toep, wc, d_sub)


def _ssm_tables(a_re, a_im, log_dt, b_re, b_im, c_re, c_im, d_skip):
    g, p, hc = SSM_GROUPS, SSM_STATE, SSM_GROUP
    lr = jnp.minimum(a_re.astype(F32), -1e-4)
    li = a_im.astype(F32)
    dt = jnp.exp(log_dt.astype(F32))[:, None]
    mag = jnp.exp(lr * dt)
    lam_re = mag * jnp.cos(li * dt)
    lam_im = mag * jnp.sin(li * dt)
    nr, ni = lam_re - 1.0, lam_im
    den = lr * lr + li * li
    coef_re = ((nr * lr + ni * li) / den)[..., None]
    coef_im = ((ni * lr - nr * li) / den)[..., None]
    br, bi = b_re.astype(F32), b_im.astype(F32)
    bbar_re = coef_re * br - coef_im * bi
    bbar_im = coef_re * bi + coef_im * br
    cr, ci = c_re.astype(F32), c_im.astype(F32)
    kk = jnp.arange(SUB + 1, dtype=F32)[:, None, None]
    pmag = jnp.exp(lr * dt * kk)
    pw_re = pmag * jnp.cos(li * dt * kk)
    pw_im = pmag * jnp.sin(li * dt * kk)
    hi = lax.Precision.HIGHEST
    cl_re = cr[None] * pw_re[:, :, None, :] - ci[None] * pw_im[:, :, None, :]
    cl_im = cr[None] * pw_im[:, :, None, :] + ci[None] * pw_re[:, :, None, :]
    kern = (jnp.einsum('kgop,gph->kgho', cl_re[:SUB], bbar_re, precision=hi)
            - jnp.einsum('kgop,gph->kgho', cl_im[:SUB], bbar_im, precision=hi))
    s_idx = jnp.arange(SUB)[:, None]
    t_idx = jnp.arange(SUB)[None, :]
    lag = t_idx - s_idx
    toep = jnp.where((lag >= 0)[None, :, :, None, None],
                     jnp.transpose(kern, (1, 0, 2, 3))[:, jnp.clip(lag, 0, SUB - 1)], 0.0)
    toep = jnp.transpose(toep, (0, 1, 3, 2, 4)).reshape(g, SUB * hc, SUB * hc)
    rev_re = pw_re[:SUB][::-1]
    rev_im = pw_im[:SUB][::-1]
    ws_re = rev_re[:, :, :, None] * bbar_re[None] - rev_im[:, :, :, None] * bbar_im[None]
    ws_im = rev_re[:, :, :, None] * bbar_im[None] + rev_im[:, :, :, None] * bbar_re[None]
    ws_re = jnp.transpose(ws_re, (1, 0, 3, 2)).reshape(g, SUB * hc, p)
    ws_im = jnp.transpose(ws_im, (1, 0, 3, 2)).reshape(g, SUB * hc, p)
    wc_re = jnp.transpose(cl_re[1:], (1, 3, 0, 2)).reshape(g, p, SUB * hc)
    wc_im = -jnp.transpose(cl_im[1:], (1, 3, 0, 2)).reshape(g, p, SUB * hc)
    z_s = jnp.zeros_like(ws_re[0::2])
    ws = jnp.concatenate([
        jnp.concatenate([ws_re[0::2], z_s, ws_im[0::2], z_s], axis=-1),
        jnp.concatenate([z_s, ws_re[1::2], z_s, ws_im[1::2]], axis=-1)], axis=1)
    z_c = jnp.zeros_like(wc_re[0::2])
    wc = jnp.concatenate([
        jnp.concatenate([wc_re[0::2], z_c], axis=-1),
        jnp.concatenate([z_c, wc_re[1::2]], axis=-1),
        jnp.concatenate([wc_im[0::2], z_c], axis=-1),
        jnp.concatenate([z_c, wc_im[1::2]], axis=-1)], axis=1)
    a16 = jnp.stack([pw_re[SUB].reshape(g // 2, 2 * p), pw_im[SUB].reshape(g // 2, 2 * p)], axis=1)
    d_sub = jnp.tile(d_skip.astype(F32).reshape(g // 2, 2, 1, hc), (1, 1, SUB, 1)).reshape(g // 2, 1, PAIR_W)
    return ws.astype(BF16), a16, toep.astype(BF16), wc.astype(BF16), d_sub


def _mix_ffn_kernel(h_ref, attn_ref, g_ref, wglu_ref, ga_ref, gs_ref, wo_ref, gffn_ref,
                    wg_ref, wu_ref, wd_ref, gfin_ref, o_ref, *, final):
    g = g_ref[...]
    ssm = g.astype(F32) * jax.nn.sigmoid(jnp.dot(g, wglu_ref[...], preferred_element_type=F32))
    a_n = _rms(attn_ref[...].astype(F32), ga_ref[...]).astype(BF16)
    s_n = _rms(ssm, gs_ref[...]).astype(BF16)
    h1 = (h_ref[...]
          + jnp.dot(a_n, wo_ref[:ATTN_WIDTH, :], preferred_element_type=F32)
          + jnp.dot(s_n, wo_ref[ATTN_WIDTH:, :], preferred_element_type=F32))
    hn = _rms(h1, gffn_ref[...]).astype(BF16)
    o_ref[...] = h1
    for c in range(FFN_HIDDEN // FFN_CHUNK):
        cols = slice(c * FFN_CHUNK, (c + 1) * FFN_CHUNK)
        gate = jnp.dot(hn, wg_ref[:, cols], preferred_element_type=F32)
        up = jnp.dot(hn, wu_ref[:, cols], preferred_element_type=F32)
        act = (jax.nn.silu(gate) * up).astype(BF16)
        o_ref[...] += jnp.dot(act, wd_ref[cols, :], preferred_element_type=F32)
    if final:
        o_ref[...] = _rms(o_ref[...], gfin_ref[...])


def _mix_ffn(h, attn, g, wglu, ga, gs, wo, gffn, wg, wu, wd, gfin, *, final):
    n = h.shape[0]
    tm = SEQ_TILE
    row = lambda i: (i, 0)
    const = lambda i: (0, 0)
    full = lambda a: pl.BlockSpec(a.shape, const, pipeline_mode=pl.Buffered(1))
    return pl.pallas_call(
        functools.partial(_mix_ffn_kernel, final=final),
        grid=(n // tm,),
        in_specs=[pl.BlockSpec((tm, D_MODEL), row), pl.BlockSpec((tm, ATTN_WIDTH), row),
                  pl.BlockSpec((tm, SSM_WIDTH), row),
                  full(wglu), full(ga), full(gs), full(wo), full(gffn),
                  full(wg), full(wu), full(wd), full(gfin)],
        out_specs=pl.BlockSpec((tm, D_MODEL), row),
        out_shape=jax.ShapeDtypeStruct((n, D_MODEL), F32),
        compiler_params=pltpu.CompilerParams(dimension_semantics=("arbitrary",),
                                             vmem_limit_bytes=VMEM_LIMIT),
        name="mix_ffn",
    )(h, attn, g, wglu, ga, gs, wo, gffn, wg, wu, wd, gfin)


def _prep_w_in(w):
    cq = w[:, :Q_LORA_RANK]
    ckv = w[:, Q_LORA_RANK:Q_LORA_RANK + KV_LORA_RANK]
    kr = w[:, Q_LORA_RANK + KV_LORA_RANK:Q_LORA_RANK + KV_LORA_RANK + QK_ROPE_DIM]
    u = w[:, Q_LORA_RANK + KV_LORA_RANK + QK_ROPE_DIM:]
    x1, x2 = kr[:, :QK_ROPE_DIM // 2], kr[:, QK_ROPE_DIM // 2:]
    return jnp.concatenate([cq, ckv, u, x1, x2, x2, x1], axis=1).astype(BF16)


def _prep_w_uq(w):
    w = w.reshape(Q_LORA_RANK, ATTN_HEADS, QK_NOPE_DIM + QK_ROPE_DIM)
    nope = w[..., :QK_NOPE_DIM]
    x1 = w[..., QK_NOPE_DIM:QK_NOPE_DIM + QK_ROPE_DIM // 2]
    x2 = w[..., QK_NOPE_DIM + QK_ROPE_DIM // 2:]
    return jnp.concatenate([nope, x1, x2, x2, x1], axis=-1).reshape(Q_LORA_RANK, ATTN_HEADS * HEAD_PAD).astype(BF16)


def _prep_w_ukv(w):
    w = w.reshape(KV_LORA_RANK, ATTN_HEADS, QK_NOPE_DIM + V_HEAD_DIM)
    k = w[..., :QK_NOPE_DIM].reshape(KV_LORA_RANK, ATTN_HEADS * QK_NOPE_DIM)
    v = w[..., QK_NOPE_DIM:].reshape(KV_LORA_RANK, ATTN_WIDTH)
    return jnp.concatenate([k, v], axis=1).astype(BF16)


def kernel(x, meta_tokens, norm_mix_g, w_in, q_norm_g, w_uq, kv_norm_g, w_ukv, ssm_a_re, ssm_a_im, ssm_log_dt, ssm_b_re, ssm_b_im, ssm_c_re, ssm_c_im, ssm_d, w_glu, attn_out_g, ssm_out_g, w_o, norm_ffn_g, w_gate, w_up, w_down, final_norm_g):
    b, seq, d = x.shape
    m = NUM_META
    length = seq + m
    lp = -(-length // SEQ_TILE) * SEQ_TILE
    depth = w_in.shape[0]
    h = jnp.concatenate([jnp.broadcast_to(meta_tokens[None].astype(x.dtype), (b, m, d)), x,
                         jnp.zeros((b, lp - length, d), x.dtype)], axis=1).reshape(b * lp, d)
    pos_f = jnp.arange(lp, dtype=F32)
    inv_freq = 1.0 / (ROPE_THETA ** (jnp.arange(0, QK_ROPE_DIM, 2, dtype=F32) / QK_ROPE_DIM))
    ang = pos_f[:, None] * inv_freq[None, :]
    cos, sin = jnp.cos(ang), jnp.sin(ang)
    zpad = jnp.zeros((lp, LANE - QK_ROPE_DIM), F32)
    ctab = jnp.concatenate([cos, cos, zpad], axis=1)
    stab = jnp.concatenate([-sin, sin, zpad], axis=1)
    steps = lp // SUB
    pairs = SSM_GROUPS // 2
    row2 = lambda v: v.reshape(1, -1).astype(F32)
    for l in range(depth):
        q, k, v, u = _proj(h, ctab, stab, row2(norm_mix_g[l]), _prep_w_in(w_in[l]),
                           row2(q_norm_g[l]), _prep_w_uq(w_uq[l]), row2(kv_norm_g[l]),
                           _prep_w_ukv(w_ukv[l]), tiles_per_seq=lp // SEQ_TILE)
        attn = _attention(q.reshape(b, lp, -1), k.reshape(b, lp, -1), v.reshape(b, lp, -1))
        ws, a16, toep, wc, d_sub = _ssm_tables(ssm_a_re[l], ssm_a_im[l], ssm_log_dt[l], ssm_b_re[l],
                                               ssm_b_im[l], ssm_c_re[l], ssm_c_im[l], ssm_d[l])
        u_sub = jnp.transpose(u.reshape(b, steps, SUB, pairs, 2, SSM_GROUP), (3, 1, 0, 4, 2, 5))
        g_sub = _ssm(u_sub.reshape(pairs, steps * b, PAIR_W), ws, a16, toep, wc, d_sub, batch=b)
        g = jnp.transpose(g_sub.reshape(pairs, steps, b, 2, SUB, SSM_GROUP), (2, 1, 4, 0, 3, 5))
        h = _mix_ffn(h, attn.reshape(b * lp, -1), g.reshape(b * lp, SSM_WIDTH),
                     w_glu[l].astype(BF16), row2(attn_out_g[l]), row2(ssm_out_g[l]),
                     w_o[l].astype(BF16), row2(norm_ffn_g[l]), w_gate[l].astype(BF16),
                     w_up[l].astype(BF16), w_down[l].astype(BF16), row2(final_norm_g),
                     final=(l == depth - 1))
    return h.reshape(b, lp, d)[:, m:length]
```

```python
import functools
import math

import jax
import jax.numpy as jnp
from jax import lax
from jax.experimental import pallas as pl
from jax.experimental.pallas import tpu as pltpu

F32 = jnp.float32
BF16 = jnp.bfloat16

D_MODEL = 1024
NUM_META = 16
ATTN_HEADS = 4
QK_NOPE_DIM = 128
QK_ROPE_DIM = 64
V_HEAD_DIM = 128
Q_LORA_RANK = 384
KV_LORA_RANK = 256
ATTN_WIDTH = ATTN_HEADS * V_HEAD_DIM
ATTN_SCALE = 1.0 / math.sqrt(QK_NOPE_DIM + QK_ROPE_DIM)
ROPE_THETA = 10000.0
SSM_WIDTH = 512
SSM_GROUP = 16
SSM_GROUPS = SSM_WIDTH // SSM_GROUP
SSM_STATE = 64
FFN_HIDDEN = 2816
RMS_EPS = 1e-6

LANE = 128
MXU = 256
HEAD_PAD = 2 * LANE
SUB = 16
LANE_GROUPS = LANE // SSM_GROUP
QUADS = SSM_WIDTH // LANE
X_W = SUB * LANE
ST_W = LANE_GROUPS * SSM_STATE
SEQ_TILE = 640
SSM_TILE_SUB = SEQ_TILE // SUB
ATTN_PARTS = 2
FFN_CHUNK = 1408
NEG_BIG = -1e30
VMEM_LIMIT = 56 * 1024 * 1024


def _rms(x, g):
    y = x * lax.rsqrt(jnp.mean(x * x, axis=-1, keepdims=True) + RMS_EPS)
    return y * g


def _proj_kernel(h_ref, ct_ref, st_ref, gmix_ref, win_ref, gq_ref, wuq_ref, gkv_ref, wukv_ref,
                 q_ref, k_ref, v_ref, u_ref):
    hn = _rms(h_ref[...], gmix_ref[...]).astype(BF16)
    z = jnp.dot(hn, win_ref[...], preferred_element_type=F32)
    cq = _rms(z[:, :Q_LORA_RANK], gq_ref[...]).astype(BF16)
    c0 = Q_LORA_RANK + KV_LORA_RANK
    ckv = _rms(z[:, Q_LORA_RANK:c0], gkv_ref[...]).astype(BF16)
    u_ref[...] = z[:, c0:c0 + SSM_WIDTH]
    ct = ct_ref[...]
    st = st_ref[...]

    def rope(a):
        return a * ct + pltpu.roll(a, 2 * (QK_ROPE_DIM // 2), 1) * st

    kr = rope(z[:, c0 + SSM_WIDTH:]).astype(BF16)
    q = jnp.dot(cq, wuq_ref[...], preferred_element_type=F32) * ATTN_SCALE
    kv = jnp.dot(ckv, wukv_ref[...], preferred_element_type=F32)
    for h in range(ATTN_HEADS):
        lo = HEAD_PAD * h
        q_ref[:, lo:lo + LANE] = q[:, lo:lo + LANE].astype(BF16)
        q_ref[:, lo + LANE:lo + HEAD_PAD] = rope(q[:, lo + LANE:lo + HEAD_PAD]).astype(BF16)
        k_ref[:, lo:lo + LANE] = kv[:, LANE * h:LANE * (h + 1)].astype(BF16)
        k_ref[:, lo + LANE:lo + HEAD_PAD] = kr
    v_ref[...] = kv[:, ATTN_WIDTH:].astype(BF16)


def _proj(h, ctab, stab, gmix, win, gq, wuq, gkv, wukv, *, tiles_per_seq):
    n = h.shape[0]
    tm = SEQ_TILE
    row = lambda i: (i, 0)
    pos = lambda i: (i % tiles_per_seq, 0)
    const = lambda i: (0, 0)
    full = lambda a: pl.BlockSpec(a.shape, const)
    return pl.pallas_call(
        _proj_kernel,
        grid=(n // tm,),
        in_specs=[pl.BlockSpec((tm, D_MODEL), row),
                  pl.BlockSpec((tm, LANE), pos), pl.BlockSpec((tm, LANE), pos),
                  full(gmix), full(win), full(gq), full(wuq), full(gkv), full(wukv)],
        out_specs=[pl.BlockSpec((tm, ATTN_HEADS * HEAD_PAD), row),
                   pl.BlockSpec((tm, ATTN_HEADS * HEAD_PAD), row),
                   pl.BlockSpec((tm, ATTN_WIDTH), row),
                   pl.BlockSpec((tm, SSM_WIDTH), row)],
        out_shape=[jax.ShapeDtypeStruct((n, ATTN_HEADS * HEAD_PAD), BF16),
                   jax.ShapeDtypeStruct((n, ATTN_HEADS * HEAD_PAD), BF16),
                   jax.ShapeDtypeStruct((n, ATTN_WIDTH), BF16),
                   jax.ShapeDtypeStruct((n, SSM_WIDTH), F32)],
        compiler_params=pltpu.CompilerParams(dimension_semantics=("arbitrary",),
                                             vmem_limit_bytes=VMEM_LIMIT),
        name="proj",
    )(h, ctab, stab, gmix, win, gq, wuq, gkv, wukv)


def _attn_kernel(q_ref, k_ref, v_ref, o_ref, m_sc, l_sc, acc_sc, *, tile, parts):
    i = pl.program_id(2)
    hq = tile // parts
    m_sc[...] = jnp.full(m_sc.shape, NEG_BIG, F32)
    l_sc[...] = jnp.zeros(l_sc.shape, F32)
    acc_sc[...] = jnp.zeros(acc_sc.shape, F32)

    def update(a, s, v):
        rows = slice(a * hq, (a + 1) * hq)
        m_prev = m_sc[rows, :]
        m_new = jnp.maximum(m_prev, jnp.max(s, axis=-1, keepdims=True))
        alpha = jnp.exp(m_prev - m_new)
        p = jnp.exp(s - m_new)
        l_sc[rows, :] = alpha * l_sc[rows, :] + jnp.sum(p, axis=-1, keepdims=True)
        acc_sc[rows, :] = alpha * acc_sc[rows, :] + jnp.dot(p.astype(BF16), v,
                                                         preferred_element_type=F32)
        m_sc[rows, :] = m_new

    def scores(a, k):
        return lax.dot_general(q_ref[0, a * hq:(a + 1) * hq, :], k, (((1,), (1,)), ((), ())),
                               preferred_element_type=F32)

    def body(j, carry):
        start = pl.multiple_of(j * tile, tile)
        k = k_ref[0, pl.ds(start, tile), :]
        v = v_ref[0, pl.ds(start, tile), :]
        for a in range(parts):
            update(a, scores(a, k), v)
        return carry

    lax.fori_loop(0, i, body, 0)
    start = pl.multiple_of(i * tile, tile)
    for a in range(parts):
        width = (a + 1) * hq
        s = scores(a, k_ref[0, pl.ds(start, width), :])
        rows = lax.broadcasted_iota(jnp.int32, s.shape, 0) + a * hq
        cols = lax.broadcasted_iota(jnp.int32, s.shape, 1)
        s = jnp.where(cols <= rows, s, NEG_BIG)
        update(a, s, v_ref[0, pl.ds(start, width), :])
    o_ref[0] = (acc_sc[...] / l_sc[...]).astype(o_ref.dtype)


def _attention(q, k, v):
    b, lp, _ = q.shape
    tile = SEQ_TILE
    return pl.pallas_call(
        functools.partial(_attn_kernel, tile=tile, parts=ATTN_PARTS),
        grid=(b, ATTN_HEADS, lp // tile),
        in_specs=[pl.BlockSpec((1, tile, HEAD_PAD), lambda bi, hi, qi: (bi, qi, hi)),
                  pl.BlockSpec((1, lp, HEAD_PAD), lambda bi, hi, qi: (bi, 0, hi)),
                  pl.BlockSpec((1, lp, V_HEAD_DIM), lambda bi, hi, qi: (bi, 0, hi))],
        out_specs=pl.BlockSpec((1, tile, V_HEAD_DIM), lambda bi, hi, qi: (bi, qi, hi)),
        out_shape=jax.ShapeDtypeStruct((b, lp, ATTN_WIDTH), BF16),
        scratch_shapes=[pltpu.VMEM((tile, 1), F32), pltpu.VMEM((tile, 1), F32),
                        pltpu.VMEM((tile, V_HEAD_DIM), F32)],
        compiler_params=pltpu.CompilerParams(
            dimension_semantics=("arbitrary", "arbitrary", "arbitrary"),
            vmem_limit_bytes=VMEM_LIMIT),
        name="attention",
    )(q, k, v)


def _ssm_kernel(u_ref, ws_ref, a_ref, t_ref, wc_ref, d_ref, o_ref, x_sc, v_sc, sin_sc, st_sc,
                *, batch, nsub):
    nslab = 2 * ST_W // LANE
    half = nslab // 2

    @pl.when(pl.program_id(1) == 0)
    def _():
        st_sc[...] = jnp.zeros(st_sc.shape, F32)

    for b in range(batch):
        for t in range(SUB):
            x_sc[b * nsub:(b + 1) * nsub, LANE * t:LANE * (t + 1)] = (
                u_ref[b, pl.ds(t, nsub, stride=SUB), :])
    x = x_sc[...].astype(BF16)
    v = jnp.dot(x, ws_ref[0], preferred_element_type=F32)
    for j in range(nslab):
        v_sc[j] = v[:, LANE * j:LANE * (j + 1)]
    a_re = [jnp.broadcast_to(a_ref[0, 0:1, LANE * j:LANE * (j + 1)], (batch, LANE)) for j in range(half)]
    a_im = [jnp.broadcast_to(a_ref[0, 1:2, LANE * j:LANE * (j + 1)], (batch, LANE)) for j in range(half)]

    def body(r, carry):
        rows = pl.ds(r, batch, stride=nsub)
        out = []
        for j in range(half):
            s_re, s_im = carry[j], carry[half + j]
            sin_sc[j, rows, :] = s_re
            sin_sc[half + j, rows, :] = s_im
            out.append((a_re[j] * s_re - a_im[j] * s_im + v_sc[j, rows, :],
                        a_re[j] * s_im + a_im[j] * s_re + v_sc[half + j, rows, :]))
        return tuple(o[0] for o in out) + tuple(o[1] for o in out)

    state = lax.fori_loop(0, nsub, body, tuple(st_sc[j] for j in range(nslab)), unroll=4)
    for j in range(nslab):
        st_sc[j] = state[j]
    s_in = jnp.concatenate([sin_sc[j] for j in range(nslab)], axis=1).astype(BF16)
    for n in range(X_W // MXU):
        cols = slice(MXU * n, MXU * (n + 1))
        kdim = MXU * (n + 1)
        y = (jnp.dot(x[:, :kdim], t_ref[0, :kdim, cols], preferred_element_type=F32)
             + jnp.dot(s_in, wc_ref[0, :, cols], preferred_element_type=F32)
             + d_ref[0, :, cols] * x_sc[:, cols])
        g = jax.nn.gelu(y)
        for b in range(batch):
            for tt in range(MXU // LANE):
                o_ref[b, pl.ds(n * (MXU // LANE) + tt, nsub, stride=SUB), :] = (
                    g[b * nsub:(b + 1) * nsub, LANE * tt:LANE * (tt + 1)])


def _ssm(u, ws, a16, toep, wc, d_x):
    batch, lp, _ = u.shape
    nsub = SSM_TILE_SUB
    rows = batch * nsub
    nslab = 2 * ST_W // LANE
    wspec = lambda a: pl.BlockSpec((1,) + a.shape[1:], lambda q, c: (q, 0, 0),
                                   pipeline_mode=pl.Buffered(1))
    return pl.pallas_call(
        functools.partial(_ssm_kernel, batch=batch, nsub=nsub),
        grid=(QUADS, lp // SEQ_TILE),
        in_specs=[pl.BlockSpec((batch, SEQ_TILE, LANE), lambda q, c: (0, c, q)),
                  wspec(ws), wspec(a16), wspec(toep), wspec(wc), wspec(d_x)],
        out_specs=pl.BlockSpec((batch, SEQ_TILE, LANE), lambda q, c: (0, c, q)),
        out_shape=jax.ShapeDtypeStruct((batch, lp, SSM_WIDTH), F32),
        scratch_shapes=[pltpu.VMEM((rows, X_W), F32), pltpu.VMEM((nslab, rows, LANE), F32),
                        pltpu.VMEM((nslab, rows, LANE), F32), pltpu.VMEM((nslab, batch, LANE), F32)],
        compiler_params=pltpu.CompilerParams(dimension_semantics=("arbitrary", "arbitrary"),
                                             vmem_limit_bytes=VMEM_LIMIT),
        name="ssm",
    )(u, ws, a16, toep, wc, d_x)


def _ssm_tables(a_re, a_im, log_dt, b_re, b_im, c_re, c_im, d_skip):
    g, p, hc = SSM_GROUPS, SSM_STATE, SSM_GROUP
    lr = jnp.minimum(a_re.astype(F32), -1e-4)
    li = a_im.astype(F32)
    dt = jnp.exp(log_dt.astype(F32))[:, None]
    mag = jnp.exp(lr * dt)
    lam_re = mag * jnp.cos(li * dt)
    lam_im = mag * jnp.sin(li * dt)
    nr, ni = lam_re - 1.0, lam_im
    den = lr * lr + li * li
    coef_re = ((nr * lr + ni * li) / den)[..., None]
    coef_im = ((ni * lr - nr * li) / den)[..., None]
    br, bi = b_re.astype(F32), b_im.astype(F32)
    bbar_re = coef_re * br - coef_im * bi
    bbar_im = coef_re * bi + coef_im * br
    cr, ci = c_re.astype(F32), c_im.astype(F32)
    kk = jnp.arange(SUB + 1, dtype=F32)[:, None, None]
    pmag = jnp.exp(lr * dt * kk)
    pw_re = pmag * jnp.cos(li * dt * kk)
    pw_im = pmag * jnp.sin(li * dt * kk)
    hi = lax.Precision.HIGHEST
    cl_re = cr[None] * pw_re[:, :, None, :] - ci[None] * pw_im[:, :, None, :]
    cl_im = cr[None] * pw_im[:, :, None, :] + ci[None] * pw_re[:, :, None, :]
    kern = (jnp.einsum('kgop,gph->kgho', cl_re[:SUB], bbar_re, precision=hi)
            - jnp.einsum('kgop,gph->kgho', cl_im[:SUB], bbar_im, precision=hi))
    eye = jnp.eye(LANE_GROUPS, dtype=F32)
    lag = jnp.arange(SUB)[None, :] - jnp.arange(SUB)[:, None]
    toep = jnp.where((lag >= 0)[None, :, :, None, None],
                     jnp.transpose(kern, (1, 0, 2, 3))[:, jnp.clip(lag, 0, SUB - 1)], 0.0)
    toep = toep.reshape(QUADS, LANE_GROUPS, SUB, SUB, hc, hc)
    toep = jnp.einsum('qgstho,gd->qsghtdo', toep, eye).reshape(QUADS, X_W, X_W)
    kr = (SUB - 1) - kk[:SUB]
    rmag = jnp.exp(lr * dt * kr)
    rev_re = rmag * jnp.cos(li * dt * kr)
    rev_im = rmag * jnp.sin(li * dt * kr)
    ws_re = rev_re[:, :, :, None] * bbar_re[None] - rev_im[:, :, :, None] * bbar_im[None]
    ws_im = rev_re[:, :, :, None] * bbar_im[None] + rev_im[:, :, :, None] * bbar_re[None]

    def ws_tile(w):
        w = w.reshape(SUB, QUADS, LANE_GROUPS, p, hc)
        return jnp.einsum('sqgph,gd->qsghdp', w, eye).reshape(QUADS, X_W, ST_W)

    ws = jnp.concatenate([ws_tile(ws_re), ws_tile(ws_im)], axis=-1)

    def wc_tile(w):
        w = w.reshape(SUB, QUADS, LANE_GROUPS, hc, p)
        return jnp.einsum('tqgop,gd->qgptdo', w, eye).reshape(QUADS, ST_W, X_W)

    wc = jnp.concatenate([wc_tile(cl_re[1:]), wc_tile(-cl_im[1:])], axis=1)
    a16 = jnp.stack([pw_re[SUB].reshape(QUADS, ST_W), pw_im[SUB].reshape(QUADS, ST_W)], axis=1)
    d_x = jnp.tile(d_skip.astype(F32).reshape(QUADS, 1, LANE), (1, 1, SUB))
    return ws.astype(BF16), a16, toep.astype(BF16), wc.astype(BF16), d_x


def _mix_ffn_kernel(h_ref, attn_ref, g_ref, wglu_ref, ga_ref, gs_ref, wo_ref, gffn_ref,
                    wg_ref, wu_ref, wd_ref, gfin_ref, o_ref, *, final):
    g = g_ref[...]
    ssm = g * jax.nn.sigmoid(jnp.dot(g.astype(BF16), wglu_ref[...], preferred_element_type=F32))
    a_n = _rms(attn_ref[...].astype(F32), ga_ref[...]).astype(BF16)
    s_n = _rms(ssm, gs_ref[...]).astype(BF16)
    h1 = (h_ref[...]
          + jnp.dot(a_n, wo_ref[:ATTN_WIDTH, :], preferred_element_type=F32)
          + jnp.dot(s_n, wo_ref[ATTN_WIDTH:, :], preferred_element_type=F32))
    hn = _rms(h1, gffn_ref[...]).astype(BF16)
    o_ref[...] = h1
    for c in range(FFN_HIDDEN // FFN_CHUNK):
        cols = slice(c * FFN_CHUNK, (c + 1) * FFN_CHUNK)
        gate = jnp.dot(hn, wg_ref[:, cols], preferred_element_type=F32)
        up = jnp.dot(hn, wu_ref[:, cols], preferred_element_type=F32)
        act = (jax.nn.silu(gate) * up).astype(BF16)
        o_ref[...] += jnp.dot(act, wd_ref[cols, :], preferred_element_type=F32)
    if final:
        o_ref[...] = _rms(o_ref[...], gfin_ref[...])


def _mix_ffn(h, attn, g, wglu, ga, gs, wo, gffn, wg, wu, wd, gfin, *, final):
    n = h.shape[0]
    tm = SEQ_TILE
    row = lambda i: (i, 0)
    const = lambda i: (0, 0)
    full = lambda a: pl.BlockSpec(a.shape, const, pipeline_mode=pl.Buffered(1))
    return pl.pallas_call(
        functools.partial(_mix_ffn_kernel, final=final),
        grid=(n // tm,),
        in_specs=[pl.BlockSpec((tm, D_MODEL), row), pl.BlockSpec((tm, ATTN_WIDTH), row),
                  pl.BlockSpec((tm, SSM_WIDTH), row),
                  full(wglu), full(ga), full(gs), full(wo), full(gffn),
                  full(wg), full(wu), full(wd), full(gfin)],
        out_specs=pl.BlockSpec((tm, D_MODEL), row),
        out_shape=jax.ShapeDtypeStruct((n, D_MODEL), F32),
        compiler_params=pltpu.CompilerParams(dimension_semantics=("arbitrary",),
                                             vmem_limit_bytes=VMEM_LIMIT),
        name="mix_ffn",
    )(h, attn, g, wglu, ga, gs, wo, gffn, wg, wu, wd, gfin)


def _prep_w_in(w):
    cq = w[:, :Q_LORA_RANK]
    ckv = w[:, Q_LORA_RANK:Q_LORA_RANK + KV_LORA_RANK]
    kr = w[:, Q_LORA_RANK + KV_LORA_RANK:Q_LORA_RANK + KV_LORA_RANK + QK_ROPE_DIM]
    u = w[:, Q_LORA_RANK + KV_LORA_RANK + QK_ROPE_DIM:]
    x1, x2 = kr[:, :QK_ROPE_DIM // 2], kr[:, QK_ROPE_DIM // 2:]
    return jnp.concatenate([cq, ckv, u, x1, x2, x2, x1], axis=1).astype(BF16)


def _prep_w_uq(w):
    w = w.reshape(Q_LORA_RANK, ATTN_HEADS, QK_NOPE_DIM + QK_ROPE_DIM)
    nope = w[..., :QK_NOPE_DIM]
    x1 = w[..., QK_NOPE_DIM:QK_NOPE_DIM + QK_ROPE_DIM // 2]
    x2 = w[..., QK_NOPE_DIM + QK_ROPE_DIM // 2:]
    return jnp.concatenate([nope, x1, x2, x2, x1], axis=-1).reshape(Q_LORA_RANK, ATTN_HEADS * HEAD_PAD).astype(BF16)


def _prep_w_ukv(w):
    w = w.reshape(KV_LORA_RANK, ATTN_HEADS, QK_NOPE_DIM + V_HEAD_DIM)
    k = w[..., :QK_NOPE_DIM].reshape(KV_LORA_RANK, ATTN_HEADS * QK_NOPE_DIM)
    v = w[..., QK_NOPE_DIM:].reshape(KV_LORA_RANK, ATTN_WIDTH)
    return jnp.concatenate([k, v], axis=1).astype(BF16)


def kernel(x, meta_tokens, norm_mix_g, w_in, q_norm_g, w_uq, kv_norm_g, w_ukv, ssm_a_re, ssm_a_im, ssm_log_dt, ssm_b_re, ssm_b_im, ssm_c_re, ssm_c_im, ssm_d, w_glu, attn_out_g, ssm_out_g, w_o, norm_ffn_g, w_gate, w_up, w_down, final_norm_g):
    b, seq, d = x.shape
    m = NUM_META
    length = seq + m
    lp = -(-length // SEQ_TILE) * SEQ_TILE
    depth = w_in.shape[0]
    h = jnp.concatenate([jnp.broadcast_to(meta_tokens[None].astype(x.dtype), (b, m, d)), x,
                         jnp.zeros((b, lp - length, d), x.dtype)], axis=1).reshape(b * lp, d)
    pos_f = jnp.arange(lp, dtype=F32)
    inv_freq = 1.0 / (ROPE_THETA ** (jnp.arange(0, QK_ROPE_DIM, 2, dtype=F32) / QK_ROPE_DIM))
    ang = pos_f[:, None] * inv_freq[None, :]
    cos, sin = jnp.cos(ang), jnp.sin(ang)
    zpad = jnp.zeros((lp, LANE - QK_ROPE_DIM), F32)
    ctab = jnp.concatenate([cos, cos, zpad], axis=1)
    stab = jnp.concatenate([-sin, sin, zpad], axis=1)
    row2 = lambda v: v.reshape(1, -1).astype(F32)
    for l in range(depth):
        q, k, v, u = _proj(h, ctab, stab, row2(norm_mix_g[l]), _prep_w_in(w_in[l]),
                           row2(q_norm_g[l]), _prep_w_uq(w_uq[l]), row2(kv_norm_g[l]),
                           _prep_w_ukv(w_ukv[l]), tiles_per_seq=lp // SEQ_TILE)
        attn = _attention(q.reshape(b, lp, -1), k.reshape(b, lp, -1), v.reshape(b, lp, -1))
        g = _ssm(u.reshape(b, lp, SSM_WIDTH),
                 *_ssm_tables(ssm_a_re[l], ssm_a_im[l], ssm_log_dt[l], ssm_b_re[l], ssm_b_im[l],
                              ssm_c_re[l], ssm_c_im[l], ssm_d[l]))
        h = _mix_ffn(h, attn.reshape(b * lp, -1), g.reshape(b * lp, SSM_WIDTH),
                     w_glu[l].astype(BF16), row2(attn_out_g[l]), row2(ssm_out_g[l]),
                     w_o[l].astype(BF16), row2(norm_ffn_g[l]), w_gate[l].astype(BF16),
                     w_up[l].astype(BF16), w_down[l].astype(BF16), row2(final_norm_g),
                     final=(l == depth - 1))
    return h.reshape(b, lp, d)[:, m:length]
```

```python
import functools
import math

import jax
import jax.numpy as jnp
from jax import lax
from jax.experimental import pallas as pl
from jax.experimental.pallas import tpu as pltpu

F32 = jnp.float32
BF16 = jnp.bfloat16

D_MODEL = 1024
NUM_META = 16
ATTN_HEADS = 4
QK_NOPE_DIM = 128
QK_ROPE_DIM = 64
V_HEAD_DIM = 128
Q_LORA_RANK = 384
KV_LORA_RANK = 256
ATTN_WIDTH = ATTN_HEADS * V_HEAD_DIM
ATTN_SCALE = 1.0 / math.sqrt(QK_NOPE_DIM + QK_ROPE_DIM)
ROPE_THETA = 10000.0
SSM_WIDTH = 512
SSM_GROUP = 16
SSM_GROUPS = SSM_WIDTH // SSM_GROUP
SSM_STATE = 64
FFN_HIDDEN = 2816
RMS_EPS = 1e-6

LANE = 128
MXU = 256
HEAD_PAD = 2 * LANE
SUB = 16
LANE_GROUPS = LANE // SSM_GROUP
QUADS = SSM_WIDTH // LANE
X_W = SUB * LANE
ST_W = LANE_GROUPS * SSM_STATE
SEQ_TILE = 640
SSM_TILE_SUB = SEQ_TILE // SUB
OUT_TILE = 512
FFN_CHUNK = 1408
NEG_BIG = -1e30
VMEM_LIMIT = 56 * 1024 * 1024


def _rms(x, g):
    y = x * lax.rsqrt(jnp.mean(x * x, axis=-1, keepdims=True) + RMS_EPS)
    return y * g


def _proj_kernel(h_ref, ct_ref, st_ref, gmix_ref, win_ref, gq_ref, wuq_ref, gkv_ref, wukv_ref,
                 q_ref, k_ref, v_ref, u_ref):
    hn = _rms(h_ref[...], gmix_ref[...]).astype(BF16)
    z = jnp.dot(hn, win_ref[...], preferred_element_type=F32)
    cq = _rms(z[:, :Q_LORA_RANK], gq_ref[...]).astype(BF16)
    c0 = Q_LORA_RANK + KV_LORA_RANK
    ckv = _rms(z[:, Q_LORA_RANK:c0], gkv_ref[...]).astype(BF16)
    u_ref[...] = z[:, c0:c0 + SSM_WIDTH]
    ct = ct_ref[...]
    st = st_ref[...]

    def rope(a):
        return a * ct + pltpu.roll(a, 2 * (QK_ROPE_DIM // 2), 1) * st

    kr = rope(z[:, c0 + SSM_WIDTH:]).astype(BF16)
    q = jnp.dot(cq, wuq_ref[...], preferred_element_type=F32) * ATTN_SCALE
    kv = jnp.dot(ckv, wukv_ref[...], preferred_element_type=F32)
    for h in range(ATTN_HEADS):
        lo = HEAD_PAD * h
        q_ref[:, lo:lo + LANE] = q[:, lo:lo + LANE].astype(BF16)
        q_ref[:, lo + LANE:lo + HEAD_PAD] = rope(q[:, lo + LANE:lo + HEAD_PAD]).astype(BF16)
        k_ref[:, lo:lo + LANE] = kv[:, LANE * h:LANE * (h + 1)].astype(BF16)
        k_ref[:, lo + LANE:lo + HEAD_PAD] = kr
    v_ref[...] = kv[:, ATTN_WIDTH:].astype(BF16)


def _proj(h, ctab, stab, gmix, win, gq, wuq, gkv, wukv, *, tiles_per_seq):
    n = h.shape[0]
    tm = SEQ_TILE
    row = lambda i: (i, 0)
    pos = lambda i: (i % tiles_per_seq, 0)
    const = lambda i: (0, 0)
    full = lambda a: pl.BlockSpec(a.shape, const)
    return pl.pallas_call(
        _proj_kernel,
        grid=(n // tm,),
        in_specs=[pl.BlockSpec((tm, D_MODEL), row),
                  pl.BlockSpec((tm, LANE), pos), pl.BlockSpec((tm, LANE), pos),
                  full(gmix), full(win), full(gq), full(wuq), full(gkv), full(wukv)],
        out_specs=[pl.BlockSpec((tm, ATTN_HEADS * HEAD_PAD), row),
                   pl.BlockSpec((tm, ATTN_HEADS * HEAD_PAD), row),
                   pl.BlockSpec((tm, ATTN_WIDTH), row),
                   pl.BlockSpec((tm, SSM_WIDTH), row)],
        out_shape=[jax.ShapeDtypeStruct((n, ATTN_HEADS * HEAD_PAD), BF16),
                   jax.ShapeDtypeStruct((n, ATTN_HEADS * HEAD_PAD), BF16),
                   jax.ShapeDtypeStruct((n, ATTN_WIDTH), BF16),
                   jax.ShapeDtypeStruct((n, SSM_WIDTH), F32)],
        compiler_params=pltpu.CompilerParams(dimension_semantics=("arbitrary",),
                                             vmem_limit_bytes=VMEM_LIMIT),
        name="proj",
    )(h, ctab, stab, gmix, win, gq, wuq, gkv, wukv)


def _attn_kernel(q_ref, k_ref, v_ref, o_ref, s_sc, m_sc, l_sc, acc_sc, *, tile):
    i = pl.program_id(2)
    q = q_ref[0]
    m_sc[...] = jnp.full(m_sc.shape, NEG_BIG, F32)
    l_sc[...] = jnp.zeros(l_sc.shape, F32)
    acc_sc[...] = jnp.zeros(acc_sc.shape, F32)

    def scores(j):
        k = k_ref[0, pl.ds(pl.multiple_of(j * tile, tile), tile), :]
        return lax.dot_general(q, k, (((1,), (1,)), ((), ())), preferred_element_type=F32)

    def update(s, j):
        v = v_ref[0, pl.ds(pl.multiple_of(j * tile, tile), tile), :]
        m_prev = m_sc[...]
        m_new = jnp.maximum(m_prev, jnp.max(s, axis=-1, keepdims=True))
        alpha = jnp.exp(m_prev - m_new)
        p = jnp.exp(s - m_new)
        l_sc[...] = alpha * l_sc[...] + jnp.sum(p, axis=-1, keepdims=True)
        acc_sc[...] = alpha * acc_sc[...] + jnp.dot(p.astype(BF16), v, preferred_element_type=F32)
        m_sc[...] = m_new

    def masked(s):
        rows = lax.broadcasted_iota(jnp.int32, s.shape, 0)
        cols = lax.broadcasted_iota(jnp.int32, s.shape, 1)
        return jnp.where(cols <= rows, s, NEG_BIG)

    s_sc[0] = scores(0)

    def body(t, carry):
        j = 2 * t
        s_sc[1] = scores(j + 1)
        update(s_sc[0], j)
        s_sc[0] = scores(j + 2)
        update(s_sc[1], j + 1)
        return carry

    lax.fori_loop(0, i // 2, body, 0)

    @pl.when(i % 2 == 0)
    def _():
        update(masked(s_sc[0]), i)

    @pl.when(i % 2 == 1)
    def _():
        s_sc[1] = scores(i)
        update(s_sc[0], i - 1)
        update(masked(s_sc[1]), i)

    o_ref[0] = (acc_sc[...] / l_sc[...]).astype(o_ref.dtype)


def _attention(q, k, v):
    b, lp, _ = q.shape
    tile = SEQ_TILE
    return pl.pallas_call(
        functools.partial(_attn_kernel, tile=tile),
        grid=(b, ATTN_HEADS, lp // tile),
        in_specs=[pl.BlockSpec((1, tile, HEAD_PAD), lambda bi, hi, qi: (bi, qi, hi)),
                  pl.BlockSpec((1, lp, HEAD_PAD), lambda bi, hi, qi: (bi, 0, hi)),
                  pl.BlockSpec((1, lp, V_HEAD_DIM), lambda bi, hi, qi: (bi, 0, hi))],
        out_specs=pl.BlockSpec((1, tile, V_HEAD_DIM), lambda bi, hi, qi: (bi, qi, hi)),
        out_shape=jax.ShapeDtypeStruct((b, lp, ATTN_WIDTH), BF16),
        scratch_shapes=[pltpu.VMEM((2, tile, tile), F32),
                        pltpu.VMEM((tile, 1), F32), pltpu.VMEM((tile, 1), F32),
                        pltpu.VMEM((tile, V_HEAD_DIM), F32)],
        compiler_params=pltpu.CompilerParams(
            dimension_semantics=("arbitrary", "arbitrary", "arbitrary"),
            vmem_limit_bytes=VMEM_LIMIT),
        name="attention",
    )(q, k, v)


def _ssm_kernel(u_ref, ws_ref, a_ref, t_ref, wc_ref, d_ref, o_ref, x_sc, v_sc, sin_sc, st_sc,
                *, batch, nsub):
    nslab = 2 * ST_W // LANE
    half = nslab // 2

    @pl.when(pl.program_id(1) == 0)
    def _():
        st_sc[...] = jnp.zeros(st_sc.shape, F32)

    for b in range(batch):
        for t in range(SUB):
            x_sc[b * nsub:(b + 1) * nsub, LANE * t:LANE * (t + 1)] = (
                u_ref[b, pl.ds(t, nsub, stride=SUB), :])
    x = x_sc[...].astype(BF16)
    v = jnp.dot(x, ws_ref[0], preferred_element_type=F32)
    for j in range(nslab):
        v_sc[j] = v[:, LANE * j:LANE * (j + 1)]
    a_re = [jnp.broadcast_to(a_ref[0, 0:1, LANE * j:LANE * (j + 1)], (batch, LANE)) for j in range(half)]
    a_im = [jnp.broadcast_to(a_ref[0, 1:2, LANE * j:LANE * (j + 1)], (batch, LANE)) for j in range(half)]

    def body(r, carry):
        rows = pl.ds(r, batch, stride=nsub)
        out = []
        for j in range(half):
            s_re, s_im = carry[j], carry[half + j]
            sin_sc[j, rows, :] = s_re
            sin_sc[half + j, rows, :] = s_im
            out.append((a_re[j] * s_re - a_im[j] * s_im + v_sc[j, rows, :],
                        a_re[j] * s_im + a_im[j] * s_re + v_sc[half + j, rows, :]))
        return tuple(o[0] for o in out) + tuple(o[1] for o in out)

    state = lax.fori_loop(0, nsub, body, tuple(st_sc[j] for j in range(nslab)), unroll=4)
    for j in range(nslab):
        st_sc[j] = state[j]
    s_in = jnp.concatenate([sin_sc[j] for j in range(nslab)], axis=1).astype(BF16)
    for n in range(X_W // MXU):
        cols = slice(MXU * n, MXU * (n + 1))
        kdim = MXU * (n + 1)
        y = (jnp.dot(x[:, :kdim], t_ref[0, :kdim, cols], preferred_element_type=F32)
             + jnp.dot(s_in, wc_ref[0, :, cols], preferred_element_type=F32)
             + d_ref[0, :, cols] * x_sc[:, cols])
        g = jax.nn.gelu(y)
        for b in range(batch):
            for tt in range(MXU // LANE):
                o_ref[b, pl.ds(n * (MXU // LANE) + tt, nsub, stride=SUB), :] = (
                    g[b * nsub:(b + 1) * nsub, LANE * tt:LANE * (tt + 1)])


def _ssm(u, ws, a16, toep, wc, d_x):
    batch, lp, _ = u.shape
    nsub = SSM_TILE_SUB
    rows = batch * nsub
    nslab = 2 * ST_W // LANE
    wspec = lambda a: pl.BlockSpec((1,) + a.shape[1:], lambda q, c: (q, 0, 0),
                                   pipeline_mode=pl.Buffered(1))
    return pl.pallas_call(
        functools.partial(_ssm_kernel, batch=batch, nsub=nsub),
        grid=(QUADS, lp // SEQ_TILE),
        in_specs=[pl.BlockSpec((batch, SEQ_TILE, LANE), lambda q, c: (0, c, q)),
                  wspec(ws), wspec(a16), wspec(toep), wspec(wc), wspec(d_x)],
        out_specs=pl.BlockSpec((batch, SEQ_TILE, LANE), lambda q, c: (0, c, q)),
        out_shape=jax.ShapeDtypeStruct((batch, lp, SSM_WIDTH), F32),
        scratch_shapes=[pltpu.VMEM((rows, X_W), F32), pltpu.VMEM((nslab, rows, LANE), F32),
                        pltpu.VMEM((nslab, rows, LANE), F32), pltpu.VMEM((nslab, batch, LANE), F32)],
        compiler_params=pltpu.CompilerParams(dimension_semantics=("arbitrary", "arbitrary"),
                                             vmem_limit_bytes=VMEM_LIMIT),
        name="ssm",
    )(u, ws, a16, toep, wc, d_x)


def _ssm_tables(a_re, a_im, log_dt, b_re, b_im, c_re, c_im, d_skip):
    g, p, hc = SSM_GROUPS, SSM_STATE, SSM_GROUP
    lr = jnp.minimum(a_re.astype(F32), -1e-4)
    li = a_im.astype(F32)
    dt = jnp.exp(log_dt.astype(F32))[:, None]
    mag = jnp.exp(lr * dt)
    lam_re = mag * jnp.cos(li * dt)
    lam_im = mag * jnp.sin(li * dt)
    nr, ni = lam_re - 1.0, lam_im
    den = lr * lr + li * li
    coef_re = ((nr * lr + ni * li) / den)[..., None]
    coef_im = ((ni * lr - nr * li) / den)[..., None]
    br, bi = b_re.astype(F32), b_im.astype(F32)
    bbar_re = coef_re * br - coef_im * bi
    bbar_im = coef_re * bi + coef_im * br
    cr, ci = c_re.astype(F32), c_im.astype(F32)
    kk = jnp.arange(SUB + 1, dtype=F32)[:, None, None]
    pmag = jnp.exp(lr * dt * kk)
    pw_re = pmag * jnp.cos(li * dt * kk)
    pw_im = pmag * jnp.sin(li * dt * kk)
    hi = lax.Precision.HIGHEST
    cl_re = cr[None] * pw_re[:, :, None, :] - ci[None] * pw_im[:, :, None, :]
    cl_im = cr[None] * pw_im[:, :, None, :] + ci[None] * pw_re[:, :, None, :]
    kern = (jnp.einsum('kgop,gph->kgho', cl_re[:SUB], bbar_re, precision=hi)
            - jnp.einsum('kgop,gph->kgho', cl_im[:SUB], bbar_im, precision=hi))
    eye = jnp.eye(LANE_GROUPS, dtype=F32)
    lag = jnp.arange(SUB)[None, :] - jnp.arange(SUB)[:, None]
    toep = jnp.where((lag >= 0)[None, :, :, None, None],
                     jnp.transpose(kern, (1, 0, 2, 3))[:, jnp.clip(lag, 0, SUB - 1)], 0.0)
    toep = jnp.transpose(toep.reshape(QUADS, LANE_GROUPS, SUB, SUB, hc, hc), (0, 2, 1, 4, 3, 5))
    toep = (toep[:, :, :, :, :, None, :] * eye[None, None, :, None, None, :, None]).astype(BF16)
    toep = toep.reshape(QUADS, X_W, X_W)
    kr = (SUB - 1) - kk[:SUB]
    rmag = jnp.exp(lr * dt * kr)
    rev_re = rmag * jnp.cos(li * dt * kr)
    rev_im = rmag * jnp.sin(li * dt * kr)
    ws_re = rev_re[:, :, :, None] * bbar_re[None] - rev_im[:, :, :, None] * bbar_im[None]
    ws_im = rev_re[:, :, :, None] * bbar_im[None] + rev_im[:, :, :, None] * bbar_re[None]

    def ws_tile(w):
        w = jnp.transpose(w.reshape(SUB, QUADS, LANE_GROUPS, p, hc), (1, 0, 2, 4, 3))
        w = (w[:, :, :, :, None, :] * eye[None, None, :, None, :, None]).astype(BF16)
        return w.reshape(QUADS, X_W, ST_W)

    ws = jnp.concatenate([ws_tile(ws_re), ws_tile(ws_im)], axis=-1)

    def wc_tile(w):
        w = jnp.transpose(w.reshape(SUB, QUADS, LANE_GROUPS, hc, p), (1, 2, 4, 0, 3))
        w = (w[:, :, :, :, None, :] * eye[None, :, None, None, :, None]).astype(BF16)
        return w.reshape(QUADS, ST_W, X_W)

    wc = jnp.concatenate([wc_tile(cl_re[1:]), wc_tile(-cl_im[1:])], axis=1)
    a16 = jnp.stack([pw_re[SUB].reshape(QUADS, ST_W), pw_im[SUB].reshape(QUADS, ST_W)], axis=1)
    d_x = jnp.tile(d_skip.astype(F32).reshape(QUADS, 1, LANE), (1, 1, SUB))
    return ws, a16, toep, wc, d_x


def _mix_ffn_kernel(h_ref, attn_ref, g_ref, wglu_ref, ga_ref, gs_ref, wo_ref, gffn_ref,
                    wg_ref, wu_ref, wd_ref, gfin_ref, o_ref, *, final):
    g = g_ref[...]
    ssm = g * jax.nn.sigmoid(jnp.dot(g.astype(BF16), wglu_ref[...], preferred_element_type=F32))
    a_n = _rms(attn_ref[...].astype(F32), ga_ref[...]).astype(BF16)
    s_n = _rms(ssm, gs_ref[...]).astype(BF16)
    h1 = (h_ref[...]
          + jnp.dot(a_n, wo_ref[:ATTN_WIDTH, :], preferred_element_type=F32)
          + jnp.dot(s_n, wo_ref[ATTN_WIDTH:, :], preferred_element_type=F32))
    hn = _rms(h1, gffn_ref[...]).astype(BF16)
    o_ref[...] = h1
    for c in range(FFN_HIDDEN // FFN_CHUNK):
        cols = slice(c * FFN_CHUNK, (c + 1) * FFN_CHUNK)
        gate = jnp.dot(hn, wg_ref[:, cols], preferred_element_type=F32)
        up = jnp.dot(hn, wu_ref[:, cols], preferred_element_type=F32)
        act = (jax.nn.silu(gate) * up).astype(BF16)
        o_ref[...] += jnp.dot(act, wd_ref[cols, :], preferred_element_type=F32)
    if final:
        o_ref[...] = _rms(o_ref[...], gfin_ref[...])


def _mix_ffn(h, attn, g, wglu, ga, gs, wo, gffn, wg, wu, wd, gfin, *, final, lp, first, count):
    n = h.shape[0]
    row = lambda i: (i, 0)
    const = lambda i: (0, 0)
    full = lambda a: pl.BlockSpec(a.shape, const, pipeline_mode=pl.Buffered(1))
    if final:
        tm = OUT_TILE
        per_seq = count // tm
        n_out = (n // lp) * count
        align = math.gcd(lp, first, tm)
        src = lambda i: (pl.multiple_of((i // per_seq) * lp + first + (i % per_seq) * tm, align), 0)
        in_rows = lambda width: pl.BlockSpec((pl.Element(tm), pl.Element(width)), src)
    else:
        tm = SEQ_TILE
        n_out = n
        in_rows = lambda width: pl.BlockSpec((tm, width), row)
    return pl.pallas_call(
        functools.partial(_mix_ffn_kernel, final=final),
        grid=(n_out // tm,),
        in_specs=[in_rows(D_MODEL), in_rows(ATTN_WIDTH), in_rows(SSM_WIDTH),
                  full(wglu), full(ga), full(gs), full(wo), full(gffn),
                  full(wg), full(wu), full(wd), full(gfin)],
        out_specs=pl.BlockSpec((tm, D_MODEL), row),
        out_shape=jax.ShapeDtypeStruct((n_out, D_MODEL), F32),
        compiler_params=pltpu.CompilerParams(dimension_semantics=("arbitrary",),
                                             vmem_limit_bytes=VMEM_LIMIT),
        name="mix_ffn",
    )(h, attn, g, wglu, ga, gs, wo, gffn, wg, wu, wd, gfin)


def _prep_w_in(w):
    cq = w[:, :Q_LORA_RANK]
    ckv = w[:, Q_LORA_RANK:Q_LORA_RANK + KV_LORA_RANK]
    kr = w[:, Q_LORA_RANK + KV_LORA_RANK:Q_LORA_RANK + KV_LORA_RANK + QK_ROPE_DIM]
    u = w[:, Q_LORA_RANK + KV_LORA_RANK + QK_ROPE_DIM:]
    x1, x2 = kr[:, :QK_ROPE_DIM // 2], kr[:, QK_ROPE_DIM // 2:]
    return jnp.concatenate([cq, ckv, u, x1, x2, x2, x1], axis=1).astype(BF16)


def _prep_w_uq(w):
    w = w.reshape(Q_LORA_RANK, ATTN_HEADS, QK_NOPE_DIM + QK_ROPE_DIM)
    nope = w[..., :QK_NOPE_DIM]
    x1 = w[..., QK_NOPE_DIM:QK_NOPE_DIM + QK_ROPE_DIM // 2]
    x2 = w[..., QK_NOPE_DIM + QK_ROPE_DIM // 2:]
    return jnp.concatenate([nope, x1, x2, x2, x1], axis=-1).reshape(Q_LORA_RANK, ATTN_HEADS * HEAD_PAD).astype(BF16)


def _prep_w_ukv(w):
    w = w.reshape(KV_LORA_RANK, ATTN_HEADS, QK_NOPE_DIM + V_HEAD_DIM)
    k = w[..., :QK_NOPE_DIM].reshape(KV_LORA_RANK, ATTN_HEADS * QK_NOPE_DIM)
    v = w[..., QK_NOPE_DIM:].reshape(KV_LORA_RANK, ATTN_WIDTH)
    return jnp.concatenate([k, v], axis=1).astype(BF16)


def kernel(x, meta_tokens, norm_mix_g, w_in, q_norm_g, w_uq, kv_norm_g, w_ukv, ssm_a_re, ssm_a_im, ssm_log_dt, ssm_b_re, ssm_b_im, ssm_c_re, ssm_c_im, ssm_d, w_glu, attn_out_g, ssm_out_g, w_o, norm_ffn_g, w_gate, w_up, w_down, final_norm_g):
    b, seq, d = x.shape
    m = NUM_META
    length = seq + m
    lp = -(-length // SEQ_TILE) * SEQ_TILE
    depth = w_in.shape[0]
    h = jnp.concatenate([jnp.broadcast_to(meta_tokens[None].astype(x.dtype), (b, m, d)), x,
                         jnp.zeros((b, lp - length, d), x.dtype)], axis=1).reshape(b * lp, d)
    pos_f = jnp.arange(lp, dtype=F32)
    inv_freq = 1.0 / (ROPE_THETA ** (jnp.arange(0, QK_ROPE_DIM, 2, dtype=F32) / QK_ROPE_DIM))
    ang = pos_f[:, None] * inv_freq[None, :]
    cos, sin = jnp.cos(ang), jnp.sin(ang)
    zpad = jnp.zeros((lp, LANE - QK_ROPE_DIM), F32)
    ctab = jnp.concatenate([cos, cos, zpad], axis=1)
    stab = jnp.concatenate([-sin, sin, zpad], axis=1)
    row2 = lambda v: v.reshape(1, -1).astype(F32)
    for l in range(depth):
        q, k, v, u = _proj(h, ctab, stab, row2(norm_mix_g[l]), _prep_w_in(w_in[l]),
                           row2(q_norm_g[l]), _prep_w_uq(w_uq[l]), row2(kv_norm_g[l]),
                           _prep_w_ukv(w_ukv[l]), tiles_per_seq=lp // SEQ_TILE)
        attn = _attention(q.reshape(b, lp, -1), k.reshape(b, lp, -1), v.reshape(b, lp, -1))
        g = _ssm(u.reshape(b, lp, SSM_WIDTH),
                 *_ssm_tables(ssm_a_re[l], ssm_a_im[l], ssm_log_dt[l], ssm_b_re[l], ssm_b_im[l],
                              ssm_c_re[l], ssm_c_im[l], ssm_d[l]))
        h = _mix_ffn(h, attn.reshape(b * lp, -1), g.reshape(b * lp, SSM_WIDTH),
                     w_glu[l].astype(BF16), row2(attn_out_g[l]), row2(ssm_out_g[l]),
                     w_o[l].astype(BF16), row2(norm_ffn_g[l]), w_gate[l].astype(BF16),
                     w_up[l].astype(BF16), w_down[l].astype(BF16), row2(final_norm_g),
                     final=(l == depth - 1), lp=lp, first=m, count=seq)
    return h.reshape(b, seq, d)
```

```python
import functools
import math

import jax
import jax.numpy as jnp
from jax import lax
from jax.experimental import pallas as pl
from jax.experimental.pallas import tpu as pltpu

F32 = jnp.float32
BF16 = jnp.bfloat16

D_MODEL = 1024
NUM_META = 16
ATTN_HEADS = 4
QK_NOPE_DIM = 128
QK_ROPE_DIM = 64
V_HEAD_DIM = 128
Q_LORA_RANK = 384
KV_LORA_RANK = 256
ATTN_WIDTH = ATTN_HEADS * V_HEAD_DIM
ATTN_SCALE = 1.0 / math.sqrt(QK_NOPE_DIM + QK_ROPE_DIM)
ROPE_THETA = 10000.0
SSM_WIDTH = 512
SSM_GROUP = 16
SSM_GROUPS = SSM_WIDTH // SSM_GROUP
SSM_STATE = 64
FFN_HIDDEN = 2816
RMS_EPS = 1e-6

LANE = 128
MXU = 256
HEAD_PAD = 2 * LANE
SUB = 16
LANE_GROUPS = LANE // SSM_GROUP
QUADS = SSM_WIDTH // LANE
X_W = SUB * LANE
ST_W = LANE_GROUPS * SSM_STATE
SEQ_TILE = 768
SSM_TILE_SUB = SEQ_TILE // SUB
SSM_PITCH = SSM_TILE_SUB + 8
LOG2E = math.log2(math.e)
OUT_TILE = 512
FFN_CHUNK = 1408
NEG_BIG = -1e30
VMEM_LIMIT = 56 * 1024 * 1024


def _rms(x, g):
    y = x * lax.rsqrt(jnp.mean(x * x, axis=-1, keepdims=True) + RMS_EPS)
    return y * g


def _proj_kernel(h_ref, ct_ref, st_ref, gmix_ref, win_ref, gq_ref, wuq_ref, gkv_ref, wuk_ref, wuvt_ref,
                 q_ref, k_ref, vt_ref, u_ref):
    hn = _rms(h_ref[...], gmix_ref[...]).astype(BF16)
    z = jnp.dot(hn, win_ref[...], preferred_element_type=F32)
    cq = _rms(z[:, :Q_LORA_RANK], gq_ref[...]).astype(BF16)
    c0 = Q_LORA_RANK + KV_LORA_RANK
    ckv = _rms(z[:, Q_LORA_RANK:c0], gkv_ref[...]).astype(BF16)
    u_ref[...] = z[:, c0:c0 + SSM_WIDTH]
    ct = ct_ref[...]
    st = st_ref[...]

    def rope(a):
        return a * ct + pltpu.roll(a, 2 * (QK_ROPE_DIM // 2), 1) * st

    kr = rope(z[:, c0 + SSM_WIDTH:]).astype(BF16)
    q = jnp.dot(cq, wuq_ref[...], preferred_element_type=F32) * (ATTN_SCALE * LOG2E)
    kn = jnp.dot(ckv, wuk_ref[...], preferred_element_type=F32)
    for h in range(ATTN_HEADS):
        lo = HEAD_PAD * h
        q_ref[:, lo:lo + LANE] = q[:, lo:lo + LANE].astype(BF16)
        q_ref[:, lo + LANE:lo + HEAD_PAD] = rope(q[:, lo + LANE:lo + HEAD_PAD]).astype(BF16)
        k_ref[:, lo:lo + LANE] = kn[:, LANE * h:LANE * (h + 1)].astype(BF16)
        k_ref[:, lo + LANE:lo + HEAD_PAD] = kr
    vt_ref[0] = lax.dot_general(wuvt_ref[...], ckv, (((1,), (1,)), ((), ())),
                                preferred_element_type=F32).astype(BF16)


def _proj(h, ctab, stab, gmix, win, gq, wuq, gkv, wuk, wuvt, *, tiles_per_seq):
    n = h.shape[0]
    tm = SEQ_TILE
    row = lambda i: (i, 0)
    pos = lambda i: (i % tiles_per_seq, 0)
    const = lambda i: (0, 0)
    full = lambda a: pl.BlockSpec(a.shape, const)
    return pl.pallas_call(
        _proj_kernel,
        grid=(n // tm,),
        in_specs=[pl.BlockSpec((tm, D_MODEL), row),
                  pl.BlockSpec((tm, LANE), pos), pl.BlockSpec((tm, LANE), pos),
                  full(gmix), full(win), full(gq), full(wuq), full(gkv), full(wuk), full(wuvt)],
        out_specs=[pl.BlockSpec((tm, ATTN_HEADS * HEAD_PAD), row),
                   pl.BlockSpec((tm, ATTN_HEADS * HEAD_PAD), row),
                   pl.BlockSpec((1, ATTN_WIDTH, tm), lambda i: (i // tiles_per_seq, 0, i % tiles_per_seq)),
                   pl.BlockSpec((tm, SSM_WIDTH), row)],
        out_shape=[jax.ShapeDtypeStruct((n, ATTN_HEADS * HEAD_PAD), BF16),
                   jax.ShapeDtypeStruct((n, ATTN_HEADS * HEAD_PAD), BF16),
                   jax.ShapeDtypeStruct((n // (tm * tiles_per_seq), ATTN_WIDTH, tm * tiles_per_seq), BF16),
                   jax.ShapeDtypeStruct((n, SSM_WIDTH), F32)],
        compiler_params=pltpu.CompilerParams(dimension_semantics=("arbitrary",),
                                             vmem_limit_bytes=VMEM_LIMIT),
        name="proj",
    )(h, ctab, stab, gmix, win, gq, wuq, gkv, wuk, wuvt)


def _attn_kernel(q_ref, k_ref, vt_ref, o_ref, s_sc, m_sc, l_sc, acc_sc, *, tile):
    i = pl.program_id(2)
    q = q_ref[0]
    m_sc[...] = jnp.full(m_sc.shape, NEG_BIG, F32)
    l_sc[...] = jnp.zeros(l_sc.shape, F32)
    acc_sc[...] = jnp.zeros(acc_sc.shape, F32)

    def scores(j):
        k = k_ref[0, pl.ds(pl.multiple_of(j * tile, tile), tile), :]
        return lax.dot_general(k, q, (((1,), (1,)), ((), ())), preferred_element_type=F32)

    def update(s, j):
        vt = vt_ref[0, :, pl.ds(pl.multiple_of(j * tile, tile), tile)]
        m_prev = m_sc[...]
        m_new = jnp.maximum(m_prev, jnp.max(s, axis=0, keepdims=True))
        alpha = jnp.exp2(m_prev - m_new)
        p = jnp.exp2(s - m_new)
        l_sc[...] = alpha * l_sc[...] + jnp.sum(p, axis=0, keepdims=True)
        acc_sc[...] = alpha * acc_sc[...] + jnp.dot(vt, p.astype(BF16), preferred_element_type=F32)
        m_sc[...] = m_new

    def masked(s):
        keys = lax.broadcasted_iota(jnp.int32, s.shape, 0)
        queries = lax.broadcasted_iota(jnp.int32, s.shape, 1)
        return jnp.where(keys <= queries, s, NEG_BIG)

    s_sc[0] = scores(0)

    def body(t, carry):
        j = 2 * t
        s_sc[1] = scores(j + 1)
        update(s_sc[0], j)
        s_sc[0] = scores(j + 2)
        update(s_sc[1], j + 1)
        return carry

    lax.fori_loop(0, i // 2, body, 0)

    @pl.when(i % 2 == 0)
    def _():
        update(masked(s_sc[0]), i)

    @pl.when(i % 2 == 1)
    def _():
        s_sc[1] = scores(i)
        update(s_sc[0], i - 1)
        update(masked(s_sc[1]), i)

    o_ref[0] = jnp.transpose(acc_sc[...] / l_sc[...]).astype(o_ref.dtype)


def _attention(q, k, vt):
    b, lp, _ = q.shape
    tile = SEQ_TILE
    return pl.pallas_call(
        functools.partial(_attn_kernel, tile=tile),
        grid=(b, ATTN_HEADS, lp // tile),
        in_specs=[pl.BlockSpec((1, tile, HEAD_PAD), lambda bi, hi, qi: (bi, qi, hi)),
                  pl.BlockSpec((1, lp, HEAD_PAD), lambda bi, hi, qi: (bi, 0, hi)),
                  pl.BlockSpec((1, V_HEAD_DIM, lp), lambda bi, hi, qi: (bi, hi, 0))],
        out_specs=pl.BlockSpec((1, tile, V_HEAD_DIM), lambda bi, hi, qi: (bi, qi, hi)),
        out_shape=jax.ShapeDtypeStruct((b, lp, ATTN_WIDTH), BF16),
        scratch_shapes=[pltpu.VMEM((2, tile, tile), F32),
                        pltpu.VMEM((1, tile), F32), pltpu.VMEM((1, tile), F32),
                        pltpu.VMEM((V_HEAD_DIM, tile), F32)],
        compiler_params=pltpu.CompilerParams(
            dimension_semantics=("arbitrary", "arbitrary", "arbitrary"),
            vmem_limit_bytes=VMEM_LIMIT),
        name="attention",
    )(q, k, vt)


def _ssm_kernel(u_ref, ws_ref, a_ref, t_ref, wc_ref, d_ref, o_ref, x_sc, v_sc, sin_sc, st_sc,
                *, batch, nsub, pitch):
    nslab = 2 * ST_W // LANE
    half = nslab // 2

    @pl.when(pl.program_id(1) == 0)
    def _():
        st_sc[...] = jnp.zeros(st_sc.shape, F32)

    @pl.when((pl.program_id(0) == 0) & (pl.program_id(1) == 0))
    def _():
        x_sc[...] = jnp.zeros(x_sc.shape, F32)
        sin_sc[...] = jnp.zeros(sin_sc.shape, F32)

    for b in range(batch):
        for t in range(SUB):
            x_sc[b * pitch:b * pitch + nsub, LANE * t:LANE * (t + 1)] = (
                u_ref[b, pl.ds(t, nsub, stride=SUB), :])
    x = x_sc[...].astype(BF16)
    v = jnp.dot(x, ws_ref[0], preferred_element_type=F32)
    for j in range(nslab):
        v_sc[j] = v[:, LANE * j:LANE * (j + 1)]
    a_re = [jnp.broadcast_to(a_ref[0, 0:1, LANE * j:LANE * (j + 1)], (batch, LANE)) for j in range(half)]
    a_im = [jnp.broadcast_to(a_ref[0, 1:2, LANE * j:LANE * (j + 1)], (batch, LANE)) for j in range(half)]

    def body(r, carry):
        rows = pl.ds(r, batch, stride=pitch)
        out = []
        for j in range(half):
            s_re, s_im = carry[j], carry[half + j]
            sin_sc[j, rows, :] = s_re
            sin_sc[half + j, rows, :] = s_im
            out.append((a_re[j] * s_re - a_im[j] * s_im + v_sc[j, rows, :],
                        a_re[j] * s_im + a_im[j] * s_re + v_sc[half + j, rows, :]))
        return tuple(o[0] for o in out) + tuple(o[1] for o in out)

    state = lax.fori_loop(0, nsub, body, tuple(st_sc[j] for j in range(nslab)), unroll=4)
    for j in range(nslab):
        st_sc[j] = state[j]
    s_in = jnp.concatenate([sin_sc[j] for j in range(nslab)], axis=1).astype(BF16)
    for n in range(X_W // MXU):
        cols = slice(MXU * n, MXU * (n + 1))
        kdim = MXU * (n + 1)
        y = (jnp.dot(x[:, :kdim], t_ref[0, :kdim, cols], preferred_element_type=F32)
             + jnp.dot(s_in, wc_ref[0, :, cols], preferred_element_type=F32)
             + d_ref[0, :, cols] * x_sc[:, cols])
        g = jax.nn.gelu(y)
        for b in range(batch):
            for tt in range(MXU // LANE):
                o_ref[b, pl.ds(n * (MXU // LANE) + tt, nsub, stride=SUB), :] = (
                    g[b * pitch:b * pitch + nsub, LANE * tt:LANE * (tt + 1)])


def _ssm(u, ws, a16, toep, wc, d_x):
    batch, lp, _ = u.shape
    nsub = SSM_TILE_SUB
    rows = batch * SSM_PITCH
    nslab = 2 * ST_W // LANE
    wspec = lambda a: pl.BlockSpec((1,) + a.shape[1:], lambda q, c: (q, 0, 0),
                                   pipeline_mode=pl.Buffered(1))
    return pl.pallas_call(
        functools.partial(_ssm_kernel, batch=batch, nsub=nsub, pitch=SSM_PITCH),
        grid=(QUADS, lp // SEQ_TILE),
        in_specs=[pl.BlockSpec((batch, SEQ_TILE, LANE), lambda q, c: (0, c, q)),
                  wspec(ws), wspec(a16), wspec(toep), wspec(wc), wspec(d_x)],
        out_specs=pl.BlockSpec((batch, SEQ_TILE, LANE), lambda q, c: (0, c, q)),
        out_shape=jax.ShapeDtypeStruct((batch, lp, SSM_WIDTH), F32),
        scratch_shapes=[pltpu.VMEM((rows, X_W), F32), pltpu.VMEM((nslab, rows, LANE), F32),
                        pltpu.VMEM((nslab, rows, LANE), F32), pltpu.VMEM((nslab, batch, LANE), F32)],
        compiler_params=pltpu.CompilerParams(dimension_semantics=("arbitrary", "arbitrary"),
                                             vmem_limit_bytes=VMEM_LIMIT),
        name="ssm",
    )(u, ws, a16, toep, wc, d_x)


def _ssm_tables(a_re, a_im, log_dt, b_re, b_im, c_re, c_im, d_skip):
    g, p, hc = SSM_GROUPS, SSM_STATE, SSM_GROUP
    lr = jnp.minimum(a_re.astype(F32), -1e-4)
    li = a_im.astype(F32)
    dt = jnp.exp(log_dt.astype(F32))[:, None]
    mag = jnp.exp(lr * dt)
    lam_re = mag * jnp.cos(li * dt)
    lam_im = mag * jnp.sin(li * dt)
    nr, ni = lam_re - 1.0, lam_im
    den = lr * lr + li * li
    coef_re = ((nr * lr + ni * li) / den)[..., None]
    coef_im = ((ni * lr - nr * li) / den)[..., None]
    br, bi = b_re.astype(F32), b_im.astype(F32)
    bbar_re = coef_re * br - coef_im * bi
    bbar_im = coef_re * bi + coef_im * br
    cr, ci = c_re.astype(F32), c_im.astype(F32)
    kk = jnp.arange(SUB + 1, dtype=F32)[:, None, None]
    pmag = jnp.exp(lr * dt * kk)
    pw_re = pmag * jnp.cos(li * dt * kk)
    pw_im = pmag * jnp.sin(li * dt * kk)
    hi = lax.Precision.HIGHEST
    cl_re = cr[None] * pw_re[:, :, None, :] - ci[None] * pw_im[:, :, None, :]
    cl_im = cr[None] * pw_im[:, :, None, :] + ci[None] * pw_re[:, :, None, :]
    kern = (jnp.einsum('kgop,gph->kgho', cl_re[:SUB], bbar_re, precision=hi)
            - jnp.einsum('kgop,gph->kgho', cl_im[:SUB], bbar_im, precision=hi))
    lag = jnp.arange(SUB)[None, :] - jnp.arange(SUB)[:, None]
    toep = jnp.where((lag >= 0)[None, :, :, None, None],
                     jnp.transpose(kern, (1, 0, 2, 3))[:, jnp.clip(lag, 0, SUB - 1)], 0.0)
    row_g = lambda n, per: (jnp.arange(n)[:, None] // per) % LANE_GROUPS
    col_g = lambda n, per: (jnp.arange(n)[None, :] // per) % LANE_GROUPS
    src = jnp.arange(SUB * hc)[:, None]
    dst = jnp.arange(X_W)[None, :]
    spread = ((src // hc == dst // LANE) & (src % hc == dst % hc)).astype(F32)
    toep = jnp.transpose(toep.reshape(QUADS, LANE_GROUPS, SUB, SUB, hc, hc), (0, 2, 1, 4, 3, 5))
    toep = jnp.einsum('qra,ac->qrc', toep.reshape(QUADS, X_W, SUB * hc), spread, precision=hi)
    toep = jnp.where(row_g(X_W, hc) == col_g(X_W, hc), toep, 0.0).astype(BF16)
    kr = (SUB - 1) - kk[:SUB]
    rmag = jnp.exp(lr * dt * kr)
    rev_re = rmag * jnp.cos(li * dt * kr)
    rev_im = rmag * jnp.sin(li * dt * kr)
    ws_re = rev_re[:, :, :, None] * bbar_re[None] - rev_im[:, :, :, None] * bbar_im[None]
    ws_im = rev_re[:, :, :, None] * bbar_im[None] + rev_im[:, :, :, None] * bbar_re[None]

    def ws_tile(w):
        w = jnp.transpose(w.reshape(SUB, QUADS, LANE_GROUPS, p, hc), (1, 0, 2, 4, 3)).reshape(QUADS, X_W, p)
        w = jnp.tile(w, (1, 1, LANE_GROUPS))
        return jnp.where(row_g(X_W, hc) == col_g(ST_W, p), w, 0.0).astype(BF16)

    ws = jnp.concatenate([ws_tile(ws_re), ws_tile(ws_im)], axis=-1)

    def wc_tile(w):
        w = jnp.transpose(w.reshape(SUB, QUADS, LANE_GROUPS, hc, p), (1, 2, 4, 0, 3)).reshape(QUADS, ST_W, SUB * hc)
        w = jnp.einsum('qra,ac->qrc', w, spread, precision=hi)
        return jnp.where(row_g(ST_W, p) == col_g(X_W, hc), w, 0.0).astype(BF16)

    wc = jnp.concatenate([wc_tile(cl_re[1:]), wc_tile(-cl_im[1:])], axis=1)
    a16 = jnp.stack([pw_re[SUB].reshape(QUADS, ST_W), pw_im[SUB].reshape(QUADS, ST_W)], axis=1)
    d_x = jnp.tile(d_skip.astype(F32).reshape(QUADS, 1, LANE), (1, 1, SUB))
    return ws, a16, toep, wc, d_x


def _mix_ffn_kernel(h_ref, attn_ref, g_ref, wglu_ref, ga_ref, gs_ref, wo_ref, gffn_ref,
                    wg_ref, wu_ref, wd_ref, gfin_ref, o_ref, *, final):
    g = g_ref[...]
    ssm = g * jax.nn.sigmoid(jnp.dot(g.astype(BF16), wglu_ref[...], preferred_element_type=F32))
    a_n = _rms(attn_ref[...].astype(F32), ga_ref[...]).astype(BF16)
    s_n = _rms(ssm, gs_ref[...]).astype(BF16)
    h1 = (h_ref[...]
          + jnp.dot(a_n, wo_ref[:ATTN_WIDTH, :], preferred_element_type=F32)
          + jnp.dot(s_n, wo_ref[ATTN_WIDTH:, :], preferred_element_type=F32))
    hn = _rms(h1, gffn_ref[...]).astype(BF16)
    o_ref[...] = h1
    for c in range(FFN_HIDDEN // FFN_CHUNK):
        cols = slice(c * FFN_CHUNK, (c + 1) * FFN_CHUNK)
        gate = jnp.dot(hn, wg_ref[:, cols], preferred_element_type=F32)
        up = jnp.dot(hn, wu_ref[:, cols], preferred_element_type=F32)
        act = (jax.nn.silu(gate) * up).astype(BF16)
        o_ref[...] += jnp.dot(act, wd_ref[cols, :], preferred_element_type=F32)
    if final:
        o_ref[...] = _rms(o_ref[...], gfin_ref[...])


def _mix_ffn(h, attn, g, wglu, ga, gs, wo, gffn, wg, wu, wd, gfin, *, final, lp, first, count):
    n = h.shape[0]
    row = lambda i: (i, 0)
    const = lambda i: (0, 0)
    full = lambda a: pl.BlockSpec(a.shape, const, pipeline_mode=pl.Buffered(1))
    if final:
        tm = OUT_TILE
        per_seq = count // tm
        n_out = (n // lp) * count
        align = math.gcd(lp, first, tm)
        src = lambda i: (pl.multiple_of((i // per_seq) * lp + first + (i % per_seq) * tm, align), 0)
        in_rows = lambda width: pl.BlockSpec((pl.Element(tm), pl.Element(width)), src)
    else:
        tm = SEQ_TILE
        n_out = n
        in_rows = lambda width: pl.BlockSpec((tm, width), row)
    return pl.pallas_call(
        functools.partial(_mix_ffn_kernel, final=final),
        grid=(n_out // tm,),
        in_specs=[in_rows(D_MODEL), in_rows(ATTN_WIDTH), in_rows(SSM_WIDTH),
                  full(wglu), full(ga), full(gs), full(wo), full(gffn),
                  full(wg), full(wu), full(wd), full(gfin)],
        out_specs=pl.BlockSpec((tm, D_MODEL), row),
        out_shape=jax.ShapeDtypeStruct((n_out, D_MODEL), F32),
        compiler_params=pltpu.CompilerParams(dimension_semantics=("arbitrary",),
                                             vmem_limit_bytes=VMEM_LIMIT),
        name="mix_ffn",
    )(h, attn, g, wglu, ga, gs, wo, gffn, wg, wu, wd, gfin)


def _prep_w_in(w):
    cq = w[:, :Q_LORA_RANK]
    ckv = w[:, Q_LORA_RANK:Q_LORA_RANK + KV_LORA_RANK]
    kr = w[:, Q_LORA_RANK + KV_LORA_RANK:Q_LORA_RANK + KV_LORA_RANK + QK_ROPE_DIM]
    u = w[:, Q_LORA_RANK + KV_LORA_RANK + QK_ROPE_DIM:]
    x1, x2 = kr[:, :QK_ROPE_DIM // 2], kr[:, QK_ROPE_DIM // 2:]
    return jnp.concatenate([cq, ckv, u, x1, x2, x2, x1], axis=1).astype(BF16)


def _prep_w_uq(w):
    w = w.reshape(Q_LORA_RANK, ATTN_HEADS, QK_NOPE_DIM + QK_ROPE_DIM)
    nope = w[..., :QK_NOPE_DIM]
    x1 = w[..., QK_NOPE_DIM:QK_NOPE_DIM + QK_ROPE_DIM // 2]
    x2 = w[..., QK_NOPE_DIM + QK_ROPE_DIM // 2:]
    return jnp.concatenate([nope, x1, x2, x2, x1], axis=-1).reshape(Q_LORA_RANK, ATTN_HEADS * HEAD_PAD).astype(BF16)


def _prep_w_ukv(w):
    w = w.reshape(KV_LORA_RANK, ATTN_HEADS, QK_NOPE_DIM + V_HEAD_DIM)
    k = w[..., :QK_NOPE_DIM].reshape(KV_LORA_RANK, ATTN_HEADS * QK_NOPE_DIM)
    v = w[..., QK_NOPE_DIM:].reshape(KV_LORA_RANK, ATTN_WIDTH)
    return k.astype(BF16), jnp.transpose(v).astype(BF16)


def kernel(x, meta_tokens, norm_mix_g, w_in, q_norm_g, w_uq, kv_norm_g, w_ukv, ssm_a_re, ssm_a_im, ssm_log_dt, ssm_b_re, ssm_b_im, ssm_c_re, ssm_c_im, ssm_d, w_glu, attn_out_g, ssm_out_g, w_o, norm_ffn_g, w_gate, w_up, w_down, final_norm_g):
    b, seq, d = x.shape
    m = NUM_META
    length = seq + m
    lp = -(-length // SEQ_TILE) * SEQ_TILE
    depth = w_in.shape[0]
    h = jnp.concatenate([jnp.broadcast_to(meta_tokens[None].astype(x.dtype), (b, m, d)), x,
                         jnp.zeros((b, lp - length, d), x.dtype)], axis=1).reshape(b * lp, d)
    pos_f = jnp.arange(lp, dtype=F32)
    inv_freq = 1.0 / (ROPE_THETA ** (jnp.arange(0, QK_ROPE_DIM, 2, dtype=F32) / QK_ROPE_DIM))
    ang = pos_f[:, None] * inv_freq[None, :]
    cos, sin = jnp.cos(ang), jnp.sin(ang)
    zpad = jnp.zeros((lp, LANE - QK_ROPE_DIM), F32)
    ctab = jnp.concatenate([cos, cos, zpad], axis=1)
    stab = jnp.concatenate([-sin, sin, zpad], axis=1)
    row2 = lambda v: v.reshape(1, -1).astype(F32)
    for l in range(depth):
        q, k, vt, u = _proj(h, ctab, stab, row2(norm_mix_g[l]), _prep_w_in(w_in[l]),
                            row2(q_norm_g[l]), _prep_w_uq(w_uq[l]), row2(kv_norm_g[l]),
                            *_prep_w_ukv(w_ukv[l]), tiles_per_seq=lp // SEQ_TILE)
        attn = _attention(q.reshape(b, lp, -1), k.reshape(b, lp, -1), vt)
        g = _ssm(u.reshape(b, lp, SSM_WIDTH),
                 *_ssm_tables(ssm_a_re[l], ssm_a_im[l], ssm_log_dt[l], ssm_b_re[l], ssm_b_im[l],
                              ssm_c_re[l], ssm_c_im[l], ssm_d[l]))
        h = _mix_ffn(h, attn.reshape(b * lp, -1), g.reshape(b * lp, SSM_WIDTH),
                     w_glu[l].astype(BF16), row2(attn_out_g[l]), row2(ssm_out_g[l]),
                     w_o[l].astype(BF16), row2(norm_ffn_g[l]), w_gate[l].astype(BF16),
                     w_up[l].astype(BF16), w_down[l].astype(BF16), row2(final_norm_g),
                     final=(l == depth - 1), lp=lp, first=m, count=seq)
    return h.reshape(b, seq, d)
```

```python
import functools
import math

import jax
import jax.numpy as jnp
from jax import lax
from jax.experimental import pallas as pl
from jax.experimental.pallas import tpu as pltpu

F32 = jnp.float32
BF16 = jnp.bfloat16

D_MODEL = 1024
NUM_META = 16
ATTN_HEADS = 4
QK_NOPE_DIM = 128
QK_ROPE_DIM = 64
V_HEAD_DIM = 128
Q_LORA_RANK = 384
KV_LORA_RANK = 256
ATTN_WIDTH = ATTN_HEADS * V_HEAD_DIM
ATTN_SCALE = 1.0 / math.sqrt(QK_NOPE_DIM + QK_ROPE_DIM)
ROPE_THETA = 10000.0
SSM_WIDTH = 512
SSM_GROUP = 16
SSM_GROUPS = SSM_WIDTH // SSM_GROUP
SSM_STATE = 64
FFN_HIDDEN = 2816
RMS_EPS = 1e-6

LANE = 128
MXU = 256
HEAD_PAD = 2 * LANE
SUB = 16
LANE_GROUPS = LANE // SSM_GROUP
QUADS = SSM_WIDTH // LANE
X_W = SUB * LANE
ST_W = LANE_GROUPS * SSM_STATE
SEQ_TILE = 768
SSM_TILE_SUB = SEQ_TILE // SUB
SSM_PITCH = SSM_TILE_SUB + 8
LOG2E = math.log2(math.e)
ATTN_STEP_HEADS = 2
OUT_TILE = 512
FFN_CHUNK = 1408
NEG_BIG = -1e30
VMEM_LIMIT = 56 * 1024 * 1024


def _rms(x, g):
    y = x * lax.rsqrt(jnp.mean(x * x, axis=-1, keepdims=True) + RMS_EPS)
    return y * g


def _proj_kernel(h_ref, ct_ref, st_ref, gmix_ref, win_ref, gq_ref, wuq_ref, gkv_ref, wuk_ref, wuvt_ref,
                 q_ref, k_ref, vt_ref, u_ref):
    hn = _rms(h_ref[...], gmix_ref[...]).astype(BF16)
    z = jnp.dot(hn, win_ref[...], preferred_element_type=F32)
    cq = _rms(z[:, :Q_LORA_RANK], gq_ref[...]).astype(BF16)
    c0 = Q_LORA_RANK + KV_LORA_RANK
    ckv = _rms(z[:, Q_LORA_RANK:c0], gkv_ref[...]).astype(BF16)
    u_ref[...] = z[:, c0:c0 + SSM_WIDTH]
    ct = ct_ref[...]
    st = st_ref[...]

    def rope(a):
        return a * ct + pltpu.roll(a, 2 * (QK_ROPE_DIM // 2), 1) * st

    kr = rope(z[:, c0 + SSM_WIDTH:]).astype(BF16)
    q = jnp.dot(cq, wuq_ref[...], preferred_element_type=F32) * (ATTN_SCALE * LOG2E)
    kn = jnp.dot(ckv, wuk_ref[...], preferred_element_type=F32)
    for h in range(ATTN_HEADS):
        lo = HEAD_PAD * h
        q_ref[:, lo:lo + LANE] = q[:, lo:lo + LANE].astype(BF16)
        q_ref[:, lo + LANE:lo + HEAD_PAD] = rope(q[:, lo + LANE:lo + HEAD_PAD]).astype(BF16)
        k_ref[:, lo:lo + LANE] = kn[:, LANE * h:LANE * (h + 1)].astype(BF16)
        k_ref[:, lo + LANE:lo + HEAD_PAD] = kr
    vt_ref[0] = lax.dot_general(wuvt_ref[...], ckv, (((1,), (1,)), ((), ())),
                                preferred_element_type=F32).astype(BF16)


def _proj(h, ctab, stab, gmix, win, gq, wuq, gkv, wuk, wuvt, *, tiles_per_seq):
    n = h.shape[0]
    tm = SEQ_TILE
    row = lambda i: (i, 0)
    pos = lambda i: (i % tiles_per_seq, 0)
    const = lambda i: (0, 0)
    full = lambda a: pl.BlockSpec(a.shape, const)
    return pl.pallas_call(
        _proj_kernel,
        grid=(n // tm,),
        in_specs=[pl.BlockSpec((tm, D_MODEL), row),
                  pl.BlockSpec((tm, LANE), pos), pl.BlockSpec((tm, LANE), pos),
                  full(gmix), full(win), full(gq), full(wuq), full(gkv), full(wuk), full(wuvt)],
        out_specs=[pl.BlockSpec((tm, ATTN_HEADS * HEAD_PAD), row),
                   pl.BlockSpec((tm, ATTN_HEADS * HEAD_PAD), row),
                   pl.BlockSpec((1, ATTN_WIDTH, tm), lambda i: (i // tiles_per_seq, 0, i % tiles_per_seq)),
                   pl.BlockSpec((tm, SSM_WIDTH), row)],
        out_shape=[jax.ShapeDtypeStruct((n, ATTN_HEADS * HEAD_PAD), BF16),
                   jax.ShapeDtypeStruct((n, ATTN_HEADS * HEAD_PAD), BF16),
                   jax.ShapeDtypeStruct((n // (tm * tiles_per_seq), ATTN_WIDTH, tm * tiles_per_seq), BF16),
                   jax.ShapeDtypeStruct((n, SSM_WIDTH), F32)],
        compiler_params=pltpu.CompilerParams(dimension_semantics=("arbitrary",),
                                             vmem_limit_bytes=VMEM_LIMIT),
        name="proj",
    )(h, ctab, stab, gmix, win, gq, wuq, gkv, wuk, wuvt)


def _attn_kernel(q_ref, k_ref, vt_ref, o_ref, s_sc, m_sc, l_sc, acc_sc, *, tile, heads):
    i = pl.program_id(2)
    m_sc[...] = jnp.full(m_sc.shape, NEG_BIG, F32)
    l_sc[...] = jnp.zeros(l_sc.shape, F32)
    acc_sc[...] = jnp.zeros(acc_sc.shape, F32)

    def scores(e, j):
        k = k_ref[0, pl.ds(pl.multiple_of(j * tile, tile), tile), HEAD_PAD * e:HEAD_PAD * (e + 1)]
        q = q_ref[0, :, HEAD_PAD * e:HEAD_PAD * (e + 1)]
        return lax.dot_general(k, q, (((1,), (1,)), ((), ())), preferred_element_type=F32)

    def update(e, s, j):
        vt = vt_ref[0, V_HEAD_DIM * e:V_HEAD_DIM * (e + 1), pl.ds(pl.multiple_of(j * tile, tile), tile)]
        m_prev = m_sc[e]
        m_new = jnp.maximum(m_prev, jnp.max(s, axis=0, keepdims=True))
        alpha = jnp.exp2(m_prev - m_new)
        p = jnp.exp2(s - m_new)
        l_sc[e] = alpha * l_sc[e] + jnp.sum(p, axis=0, keepdims=True)
        acc_sc[e] = alpha * acc_sc[e] + jnp.dot(vt, p.astype(BF16), preferred_element_type=F32)
        m_sc[e] = m_new

    def masked(s):
        keys = lax.broadcasted_iota(jnp.int32, s.shape, 0)
        queries = lax.broadcasted_iota(jnp.int32, s.shape, 1)
        return jnp.where(keys <= queries, s, NEG_BIG)

    for e in range(heads):
        s_sc[e, 0] = scores(e, 0)

    def body(t, carry):
        j = 2 * t
        for e in range(heads):
            s_sc[e, 1] = scores(e, j + 1)
            update(e, s_sc[e, 0], j)
        for e in range(heads):
            s_sc[e, 0] = scores(e, j + 2)
            update(e, s_sc[e, 1], j + 1)
        return carry

    lax.fori_loop(0, i // 2, body, 0)

    @pl.when(i % 2 == 0)
    def _():
        for e in range(heads):
            update(e, masked(s_sc[e, 0]), i)

    @pl.when(i % 2 == 1)
    def _():
        for e in range(heads):
            s_sc[e, 1] = scores(e, i)
            update(e, s_sc[e, 0], i - 1)
        for e in range(heads):
            update(e, masked(s_sc[e, 1]), i)

    for e in range(heads):
        o_ref[0, :, V_HEAD_DIM * e:V_HEAD_DIM * (e + 1)] = (
            jnp.transpose(acc_sc[e] / l_sc[e]).astype(o_ref.dtype))


def _attention(q, k, vt):
    b, lp, _ = q.shape
    tile = SEQ_TILE
    hs = ATTN_STEP_HEADS
    return pl.pallas_call(
        functools.partial(_attn_kernel, tile=tile, heads=hs),
        grid=(b, ATTN_HEADS // hs, lp // tile),
        in_specs=[pl.BlockSpec((1, tile, hs * HEAD_PAD), lambda bi, hi, qi: (bi, qi, hi)),
                  pl.BlockSpec((1, lp, hs * HEAD_PAD), lambda bi, hi, qi: (bi, 0, hi)),
                  pl.BlockSpec((1, hs * V_HEAD_DIM, lp), lambda bi, hi, qi: (bi, hi, 0))],
        out_specs=pl.BlockSpec((1, tile, hs * V_HEAD_DIM), lambda bi, hi, qi: (bi, qi, hi)),
        out_shape=jax.ShapeDtypeStruct((b, lp, ATTN_WIDTH), BF16),
        scratch_shapes=[pltpu.VMEM((hs, 2, tile, tile), F32),
                        pltpu.VMEM((hs, 1, tile), F32), pltpu.VMEM((hs, 1, tile), F32),
                        pltpu.VMEM((hs, V_HEAD_DIM, tile), F32)],
        compiler_params=pltpu.CompilerParams(
            dimension_semantics=("arbitrary", "arbitrary", "arbitrary"),
            vmem_limit_bytes=VMEM_LIMIT),
        name="attention",
    )(q, k, vt)


def _ssm_kernel(u_ref, ws_ref, a_ref, t_ref, wc_ref, d_ref, o_ref, x_sc, v_sc, sin_sc, st_sc,
                *, batch, nsub, pitch):
    nslab = 2 * ST_W // LANE
    half = nslab // 2

    @pl.when(pl.program_id(1) == 0)
    def _():
        st_sc[...] = jnp.zeros(st_sc.shape, F32)

    @pl.when((pl.program_id(0) == 0) & (pl.program_id(1) == 0))
    def _():
        x_sc[...] = jnp.zeros(x_sc.shape, F32)
        sin_sc[...] = jnp.zeros(sin_sc.shape, F32)

    for b in range(batch):
        for t in range(SUB):
            x_sc[b * pitch:b * pitch + nsub, LANE * t:LANE * (t + 1)] = (
                u_ref[b, pl.ds(t, nsub, stride=SUB), :])
    x = x_sc[...].astype(BF16)
    v = jnp.dot(x, ws_ref[0], preferred_element_type=F32)
    for j in range(nslab):
        v_sc[j] = v[:, LANE * j:LANE * (j + 1)]
    a_re = [jnp.broadcast_to(a_ref[0, 0:1, LANE * j:LANE * (j + 1)], (batch, LANE)) for j in range(half)]
    a_im = [jnp.broadcast_to(a_ref[0, 1:2, LANE * j:LANE * (j + 1)], (batch, LANE)) for j in range(half)]

    def body(r, carry):
        rows = pl.ds(r, batch, stride=pitch)
        out = []
        for j in range(half):
            s_re, s_im = carry[j], carry[half + j]
            sin_sc[j, rows, :] = s_re
            sin_sc[half + j, rows, :] = s_im
            out.append((a_re[j] * s_re - a_im[j] * s_im + v_sc[j, rows, :],
                        a_re[j] * s_im + a_im[j] * s_re + v_sc[half + j, rows, :]))
        return tuple(o[0] for o in out) + tuple(o[1] for o in out)

    state = lax.fori_loop(0, nsub, body, tuple(st_sc[j] for j in range(nslab)), unroll=4)
    for j in range(nslab):
        st_sc[j] = state[j]
    s_in = jnp.concatenate([sin_sc[j] for j in range(nslab)], axis=1).astype(BF16)
    for n in range(X_W // MXU):
        cols = slice(MXU * n, MXU * (n + 1))
        kdim = MXU * (n + 1)
        y = (jnp.dot(x[:, :kdim], t_ref[0, :kdim, cols], preferred_element_type=F32)
             + jnp.dot(s_in, wc_ref[0, :, cols], preferred_element_type=F32)
             + d_ref[0, :, cols] * x_sc[:, cols])
        g = jax.nn.gelu(y)
        for b in range(batch):
            for tt in range(MXU // LANE):
                o_ref[b, pl.ds(n * (MXU // LANE) + tt, nsub, stride=SUB), :] = (
                    g[b * pitch:b * pitch + nsub, LANE * tt:LANE * (tt + 1)])


def _ssm(u, ws, a16, toep, wc, d_x):
    batch, lp, _ = u.shape
    nsub = SSM_TILE_SUB
    rows = batch * SSM_PITCH
    nslab = 2 * ST_W // LANE
    wspec = lambda a: pl.BlockSpec((1,) + a.shape[1:], lambda q, c: (q, 0, 0),
                                   pipeline_mode=pl.Buffered(1))
    return pl.pallas_call(
        functools.partial(_ssm_kernel, batch=batch, nsub=nsub, pitch=SSM_PITCH),
        grid=(QUADS, lp // SEQ_TILE),
        in_specs=[pl.BlockSpec((batch, SEQ_TILE, LANE), lambda q, c: (0, c, q)),
                  wspec(ws), wspec(a16), wspec(toep), wspec(wc), wspec(d_x)],
        out_specs=pl.BlockSpec((batch, SEQ_TILE, LANE), lambda q, c: (0, c, q)),
        out_shape=jax.ShapeDtypeStruct((batch, lp, SSM_WIDTH), F32),
        scratch_shapes=[pltpu.VMEM((rows, X_W), F32), pltpu.VMEM((nslab, rows, LANE), F32),
                        pltpu.VMEM((nslab, rows, LANE), F32), pltpu.VMEM((nslab, batch, LANE), F32)],
        compiler_params=pltpu.CompilerParams(dimension_semantics=("arbitrary", "arbitrary"),
                                             vmem_limit_bytes=VMEM_LIMIT),
        name="ssm",
    )(u, ws, a16, toep, wc, d_x)


def _ssm_tables(a_re, a_im, log_dt, b_re, b_im, c_re, c_im, d_skip):
    g, p, hc = SSM_GROUPS, SSM_STATE, SSM_GROUP
    hi = lax.Precision.HIGHEST
    lr = jnp.minimum(a_re.astype(F32), -1e-4)
    li = a_im.astype(F32)
    dt = jnp.exp(log_dt.astype(F32))[:, None]
    mag = jnp.exp(lr * dt)
    lam_re = mag * jnp.cos(li * dt)
    lam_im = mag * jnp.sin(li * dt)
    nr, ni = lam_re - 1.0, lam_im
    den = lr * lr + li * li
    coef_re = (nr * lr + ni * li) / den
    coef_im = (ni * lr - nr * li) / den
    br = jnp.transpose(b_re.astype(F32), (0, 2, 1))
    bi = jnp.transpose(b_im.astype(F32), (0, 2, 1))
    bbar_re = coef_re[:, None, :] * br - coef_im[:, None, :] * bi
    bbar_im = coef_re[:, None, :] * bi + coef_im[:, None, :] * br
    cr, ci = c_re.astype(F32), c_im.astype(F32)
    kk = jnp.arange(SUB + 1, dtype=F32)[None, :, None]
    ph = (li * dt)[:, None, :] * kk
    pmag = jnp.exp((lr * dt)[:, None, :] * kk)
    pw_re = pmag * jnp.cos(ph)
    pw_im = pmag * jnp.sin(ph)

    iota = lambda n: jnp.arange(n)
    row_g = lambda n, per: (iota(n)[:, None] // per) % LANE_GROUPS
    col_g = lambda n, per: (iota(n)[None, :] // per) % LANE_GROUPS
    kw = SUB * hc
    spread = ((iota(kw)[:, None] // hc == iota(X_W)[None, :] // LANE)
              & (iota(kw)[:, None] % hc == iota(X_W)[None, :] % hc)).astype(F32)

    cl_re = cr[:, None] * pw_re[:, :SUB, None, :] - ci[:, None] * pw_im[:, :SUB, None, :]
    cl_im = cr[:, None] * pw_im[:, :SUB, None, :] + ci[:, None] * pw_re[:, :SUB, None, :]
    kern = (jnp.einsum('ghp,gkop->ghko', bbar_re, cl_re, precision=hi)
            - jnp.einsum('ghp,gkop->ghko', bbar_im, cl_im, precision=hi)).reshape(g * hc, kw)
    shift = ((iota(kw)[None, :, None] // hc + iota(SUB)[:, None, None] == iota(kw)[None, None, :] // hc)
             & (iota(kw)[None, :, None] % hc == iota(kw)[None, None, :] % hc)).astype(F32)
    toep = jnp.einsum('ra,sac->src', kern, shift, precision=hi)
    toep = jnp.transpose(toep.reshape(SUB, QUADS, LANE, kw), (1, 0, 2, 3)).reshape(QUADS, X_W, kw)
    toep = jnp.einsum('qra,ac->qrc', toep, spread, precision=hi)
    toep = jnp.where(row_g(X_W, hc) == col_g(X_W, hc), toep, 0.0).astype(BF16)

    kr = (SUB - 1) - kk[:, :SUB]
    rph = (li * dt)[:, None, :] * kr
    rmag = jnp.exp((lr * dt)[:, None, :] * kr)
    rv_re = jnp.transpose(rmag * jnp.cos(rph), (1, 0, 2))
    rv_im = jnp.transpose(rmag * jnp.sin(rph), (1, 0, 2))

    def ws_tile(w):
        w = jnp.transpose(w.reshape(SUB, QUADS, LANE, p), (1, 0, 2, 3)).reshape(QUADS, X_W, p)
        w = jnp.tile(w, (1, 1, LANE_GROUPS))
        return jnp.where(row_g(X_W, hc) == col_g(ST_W, p), w, 0.0).astype(BF16)

    ws = jnp.concatenate(
        [ws_tile(rv_re[:, :, None, :] * bbar_re[None] - rv_im[:, :, None, :] * bbar_im[None]),
         ws_tile(rv_re[:, :, None, :] * bbar_im[None] + rv_im[:, :, None, :] * bbar_re[None])],
        axis=-1)

    rep_t = (iota(SUB)[:, None] == iota(kw)[None, :] // hc).astype(F32)
    rep_o = (iota(hc)[:, None] == iota(kw)[None, :] % hc).astype(F32)
    c_t = lambda c: jnp.einsum('gpo,oc->gpc', jnp.transpose(c, (0, 2, 1)), rep_o, precision=hi)
    l_t = lambda w: jnp.einsum('gpt,tc->gpc', jnp.transpose(w[:, 1:], (0, 2, 1)), rep_t, precision=hi)
    cr_t, ci_t, lr_t, li_t = c_t(cr), c_t(ci), l_t(pw_re), l_t(pw_im)

    def wc_tile(w):
        w = jnp.einsum('qra,ac->qrc', w.reshape(QUADS, ST_W, kw), spread, precision=hi)
        return jnp.where(row_g(ST_W, p) == col_g(X_W, hc), w, 0.0).astype(BF16)

    wc = jnp.concatenate([wc_tile(cr_t * lr_t - ci_t * li_t),
                          wc_tile(-(cr_t * li_t + ci_t * lr_t))], axis=1)
    a16 = jnp.stack([pw_re[:, SUB].reshape(QUADS, ST_W), pw_im[:, SUB].reshape(QUADS, ST_W)], axis=1)
    d_x = jnp.tile(d_skip.astype(F32).reshape(QUADS, 1, LANE), (1, 1, SUB))
    return ws, a16, toep, wc, d_x


def _mix_ffn_kernel(h_ref, attn_ref, g_ref, wglu_ref, ga_ref, gs_ref, wo_ref, gffn_ref,
                    wg_ref, wu_ref, wd_ref, gfin_ref, o_ref, *, final):
    g = g_ref[...]
    ssm = g * jax.nn.sigmoid(jnp.dot(g.astype(BF16), wglu_ref[...], preferred_element_type=F32))
    a_n = _rms(attn_ref[...].astype(F32), ga_ref[...]).astype(BF16)
    s_n = _rms(ssm, gs_ref[...]).astype(BF16)
    h1 = (h_ref[...]
          + jnp.dot(a_n, wo_ref[:ATTN_WIDTH, :], preferred_element_type=F32)
          + jnp.dot(s_n, wo_ref[ATTN_WIDTH:, :], preferred_element_type=F32))
    hn = _rms(h1, gffn_ref[...]).astype(BF16)
    o_ref[...] = h1
    for c in range(FFN_HIDDEN // FFN_CHUNK):
        cols = slice(c * FFN_CHUNK, (c + 1) * FFN_CHUNK)
        gate = jnp.dot(hn, wg_ref[:, cols], preferred_element_type=F32)
        up = jnp.dot(hn, wu_ref[:, cols], preferred_element_type=F32)
        act = (jax.nn.silu(gate) * up).astype(BF16)
        o_ref[...] += jnp.dot(act, wd_ref[cols, :], preferred_element_type=F32)
    if final:
        o_ref[...] = _rms(o_ref[...], gfin_ref[...])


def _mix_ffn(h, attn, g, wglu, ga, gs, wo, gffn, wg, wu, wd, gfin, *, final, lp, first, count):
    n = h.shape[0]
    row = lambda i: (i, 0)
    const = lambda i: (0, 0)
    full = lambda a: pl.BlockSpec(a.shape, const, pipeline_mode=pl.Buffered(1))
    if final:
        tm = OUT_TILE
        per_seq = count // tm
        n_out = (n // lp) * count
        align = math.gcd(lp, first, tm)
        src = lambda i: (pl.multiple_of((i // per_seq) * lp + first + (i % per_seq) * tm, align), 0)
        in_rows = lambda width: pl.BlockSpec((pl.Element(tm), pl.Element(width)), src)
    else:
        tm = SEQ_TILE
        n_out = n
        in_rows = lambda width: pl.BlockSpec((tm, width), row)
    return pl.pallas_call(
        functools.partial(_mix_ffn_kernel, final=final),
        grid=(n_out // tm,),
        in_specs=[in_rows(D_MODEL), in_rows(ATTN_WIDTH), in_rows(SSM_WIDTH),
                  full(wglu), full(ga), full(gs), full(wo), full(gffn),
                  full(wg), full(wu), full(wd), full(gfin)],
        out_specs=pl.BlockSpec((tm, D_MODEL), row),
        out_shape=jax.ShapeDtypeStruct((n_out, D_MODEL), F32),
        compiler_params=pltpu.CompilerParams(dimension_semantics=("arbitrary",),
                                             vmem_limit_bytes=VMEM_LIMIT),
        name="mix_ffn",
    )(h, attn, g, wglu, ga, gs, wo, gffn, wg, wu, wd, gfin)


def _prep_w_in(w):
    cq = w[:, :Q_LORA_RANK]
    ckv = w[:, Q_LORA_RANK:Q_LORA_RANK + KV_LORA_RANK]
    kr = w[:, Q_LORA_RANK + KV_LORA_RANK:Q_LORA_RANK + KV_LORA_RANK + QK_ROPE_DIM]
    u = w[:, Q_LORA_RANK + KV_LORA_RANK + QK_ROPE_DIM:]
    x1, x2 = kr[:, :QK_ROPE_DIM // 2], kr[:, QK_ROPE_DIM // 2:]
    return jnp.concatenate([cq, ckv, u, x1, x2, x2, x1], axis=1).astype(BF16)


def _prep_w_uq(w):
    w = w.reshape(Q_LORA_RANK, ATTN_HEADS, QK_NOPE_DIM + QK_ROPE_DIM)
    nope = w[..., :QK_NOPE_DIM]
    x1 = w[..., QK_NOPE_DIM:QK_NOPE_DIM + QK_ROPE_DIM // 2]
    x2 = w[..., QK_NOPE_DIM + QK_ROPE_DIM // 2:]
    return jnp.concatenate([nope, x1, x2, x2, x1], axis=-1).reshape(Q_LORA_RANK, ATTN_HEADS * HEAD_PAD).astype(BF16)


def _prep_w_ukv(w):
    w = w.reshape(KV_LORA_RANK, ATTN_HEADS, QK_NOPE_DIM + V_HEAD_DIM)
    k = w[..., :QK_NOPE_DIM].reshape(KV_LORA_RANK, ATTN_HEADS * QK_NOPE_DIM)
    v = w[..., QK_NOPE_DIM:].reshape(KV_LORA_RANK, ATTN_WIDTH)
    return k.astype(BF16), jnp.transpose(v).astype(BF16)


def kernel(x, meta_tokens, norm_mix_g, w_in, q_norm_g, w_uq, kv_norm_g, w_ukv, ssm_a_re, ssm_a_im, ssm_log_dt, ssm_b_re, ssm_b_im, ssm_c_re, ssm_c_im, ssm_d, w_glu, attn_out_g, ssm_out_g, w_o, norm_ffn_g, w_gate, w_up, w_down, final_norm_g):
    b, seq, d = x.shape
    m = NUM_META
    length = seq + m
    lp = -(-length // SEQ_TILE) * SEQ_TILE
    depth = w_in.shape[0]
    h = jnp.concatenate([jnp.broadcast_to(meta_tokens[None].astype(x.dtype), (b, m, d)), x,
                         jnp.zeros((b, lp - length, d), x.dtype)], axis=1).reshape(b * lp, d)
    pos_f = jnp.arange(lp, dtype=F32)
    inv_freq = 1.0 / (ROPE_THETA ** (jnp.arange(0, QK_ROPE_DIM, 2, dtype=F32) / QK_ROPE_DIM))
    ang = pos_f[:, None] * inv_freq[None, :]
    cos, sin = jnp.cos(ang), jnp.sin(ang)
    zpad = jnp.zeros((lp, LANE - QK_ROPE_DIM), F32)
    ctab = jnp.concatenate([cos, cos, zpad], axis=1)
    stab = jnp.concatenate([-sin, sin, zpad], axis=1)
    row2 = lambda v: v.reshape(1, -1).astype(F32)
    for l in range(depth):
        q, k, vt, u = _proj(h, ctab, stab, row2(norm_mix_g[l]), _prep_w_in(w_in[l]),
                            row2(q_norm_g[l]), _prep_w_uq(w_uq[l]), row2(kv_norm_g[l]),
                            *_prep_w_ukv(w_ukv[l]), tiles_per_seq=lp // SEQ_TILE)
        attn = _attention(q.reshape(b, lp, -1), k.reshape(b, lp, -1), vt)
        g = _ssm(u.reshape(b, lp, SSM_WIDTH),
                 *_ssm_tables(ssm_a_re[l], ssm_a_im[l], ssm_log_dt[l], ssm_b_re[l], ssm_b_im[l],
                              ssm_c_re[l], ssm_c_im[l], ssm_d[l]))
        h = _mix_ffn(h, attn.reshape(b * lp, -1), g.reshape(b * lp, SSM_WIDTH),
                     w_glu[l].astype(BF16), row2(attn_out_g[l]), row2(ssm_out_g[l]),
                     w_o[l].astype(BF16), row2(norm_ffn_g[l]), w_gate[l].astype(BF16),
                     w_up[l].astype(BF16), w_down[l].astype(BF16), row2(final_norm_g),
                     final=(l == depth - 1), lp=lp, first=m, count=seq)
    return h.reshape(b, seq, d)
```

```python
import functools
import math

import jax
import jax.numpy as jnp
from jax import lax
from jax.experimental import pallas as pl
from jax.experimental.pallas import tpu as pltpu

F32 = jnp.float32
BF16 = jnp.bfloat16

D_MODEL = 1024
NUM_META = 16
ATTN_HEADS = 4
QK_NOPE_DIM = 128
QK_ROPE_DIM = 64
V_HEAD_DIM = 128
Q_LORA_RANK = 384
KV_LORA_RANK = 256
ATTN_WIDTH = ATTN_HEADS * V_HEAD_DIM
ATTN_SCALE = 1.0 / math.sqrt(QK_NOPE_DIM + QK_ROPE_DIM)
ROPE_THETA = 10000.0
SSM_WIDTH = 512
SSM_GROUP = 16
SSM_GROUPS = SSM_WIDTH // SSM_GROUP
SSM_STATE = 64
FFN_HIDDEN = 2816
RMS_EPS = 1e-6

LANE = 128
MXU = 256
HEAD_PAD = 2 * LANE
SUB = 16
PAIR = 2 * SSM_GROUP
PAIRS = LANE // PAIR
QUADS = SSM_WIDTH // LANE
X_W = SUB * PAIR
ST_W = 2 * SSM_STATE
SEQ_TILE = 768
SSM_TILE_SUB = SEQ_TILE // SUB
SSM_PITCH = SSM_TILE_SUB + 8
LOG2E = math.log2(math.e)
ATTN_STEP_HEADS = 2
OUT_TILE = 512
NEG_BIG = -1e30
VMEM_LIMIT = 56 * 1024 * 1024


def _rms(x, g):
    y = x * lax.rsqrt(jnp.mean(x * x, axis=-1, keepdims=True) + RMS_EPS)
    return y * g


def _proj_kernel(h_ref, ct_ref, st_ref, gmix_ref, win_ref, gq_ref, wuq_ref, gkv_ref, wuk_ref, wuvt_ref,
                 q_ref, k_ref, vt_ref, u_ref):
    hn = _rms(h_ref[...], gmix_ref[...]).astype(BF16)
    z = jnp.dot(hn, win_ref[...], preferred_element_type=F32)
    cq = _rms(z[:, :Q_LORA_RANK], gq_ref[...]).astype(BF16)
    c0 = Q_LORA_RANK + KV_LORA_RANK
    ckv = _rms(z[:, Q_LORA_RANK:c0], gkv_ref[...]).astype(BF16)
    u_ref[...] = z[:, c0:c0 + SSM_WIDTH]
    ct = ct_ref[...]
    st = st_ref[...]

    def rope(a):
        return a * ct + pltpu.roll(a, 2 * (QK_ROPE_DIM // 2), 1) * st

    kr = rope(z[:, c0 + SSM_WIDTH:]).astype(BF16)
    q = jnp.dot(cq, wuq_ref[...], preferred_element_type=F32) * (ATTN_SCALE * LOG2E)
    kn = jnp.dot(ckv, wuk_ref[...], preferred_element_type=F32)
    for h in range(ATTN_HEADS):
        lo = HEAD_PAD * h
        q_ref[:, lo:lo + LANE] = q[:, lo:lo + LANE].astype(BF16)
        q_ref[:, lo + LANE:lo + HEAD_PAD] = rope(q[:, lo + LANE:lo + HEAD_PAD]).astype(BF16)
        k_ref[:, lo:lo + LANE] = kn[:, LANE * h:LANE * (h + 1)].astype(BF16)
        k_ref[:, lo + LANE:lo + HEAD_PAD] = kr
    vt_ref[0] = lax.dot_general(wuvt_ref[...], ckv, (((1,), (1,)), ((), ())),
                                preferred_element_type=F32).astype(BF16)


def _proj(h, ctab, stab, gmix, win, gq, wuq, gkv, wuk, wuvt, *, tiles_per_seq):
    n = h.shape[0]
    tm = SEQ_TILE
    row = lambda i: (i, 0)
    pos = lambda i: (i % tiles_per_seq, 0)
    const = lambda i: (0, 0)
    full = lambda a: pl.BlockSpec(a.shape, const)
    return pl.pallas_call(
        _proj_kernel,
        grid=(n // tm,),
        in_specs=[pl.BlockSpec((tm, D_MODEL), row),
                  pl.BlockSpec((tm, LANE), pos), pl.BlockSpec((tm, LANE), pos),
                  full(gmix), full(win), full(gq), full(wuq), full(gkv), full(wuk), full(wuvt)],
        out_specs=[pl.BlockSpec((tm, ATTN_HEADS * HEAD_PAD), row),
                   pl.BlockSpec((tm, ATTN_HEADS * HEAD_PAD), row),
                   pl.BlockSpec((1, ATTN_WIDTH, tm), lambda i: (i // tiles_per_seq, 0, i % tiles_per_seq)),
                   pl.BlockSpec((tm, SSM_WIDTH), row)],
        out_shape=[jax.ShapeDtypeStruct((n, ATTN_HEADS * HEAD_PAD), BF16),
                   jax.ShapeDtypeStruct((n, ATTN_HEADS * HEAD_PAD), BF16),
                   jax.ShapeDtypeStruct((n // (tm * tiles_per_seq), ATTN_WIDTH, tm * tiles_per_seq), BF16),
                   jax.ShapeDtypeStruct((n, SSM_WIDTH), F32)],
        compiler_params=pltpu.CompilerParams(dimension_semantics=("arbitrary",),
                                             vmem_limit_bytes=VMEM_LIMIT),
        name="proj",
    )(h, ctab, stab, gmix, win, gq, wuq, gkv, wuk, wuvt)


def _attn_kernel(q_ref, k_ref, vt_ref, o_ref, s_sc, m_sc, l_sc, acc_sc, *, tile, heads):
    i = pl.program_id(2)
    m_sc[...] = jnp.full(m_sc.shape, NEG_BIG, F32)
    l_sc[...] = jnp.zeros(l_sc.shape, F32)
    acc_sc[...] = jnp.zeros(acc_sc.shape, F32)

    def scores(e, j):
        k = k_ref[0, pl.ds(pl.multiple_of(j * tile, tile), tile), HEAD_PAD * e:HEAD_PAD * (e + 1)]
        q = q_ref[0, :, HEAD_PAD * e:HEAD_PAD * (e + 1)]
        return lax.dot_general(k, q, (((1,), (1,)), ((), ())), preferred_element_type=F32)

    def update(e, s, j):
        vt = vt_ref[0, V_HEAD_DIM * e:V_HEAD_DIM * (e + 1), pl.ds(pl.multiple_of(j * tile, tile), tile)]
        m_prev = m_sc[e]
        m_new = jnp.maximum(m_prev, jnp.max(s, axis=0, keepdims=True))
        alpha = jnp.exp2(m_prev - m_new)
        p = jnp.exp2(s - m_new)
        l_sc[e] = alpha * l_sc[e] + jnp.sum(p, axis=0, keepdims=True)
        acc_sc[e] = alpha * acc_sc[e] + jnp.dot(vt, p.astype(BF16), preferred_element_type=F32)
        m_sc[e] = m_new

    def masked(s):
        keys = lax.broadcasted_iota(jnp.int32, s.shape, 0)
        queries = lax.broadcasted_iota(jnp.int32, s.shape, 1)
        return jnp.where(keys <= queries, s, NEG_BIG)

    for e in range(heads):
        s_sc[e, 0] = scores(e, 0)

    def body(t, carry):
        j = 2 * t
        for e in range(heads):
            s_sc[e, 1] = scores(e, j + 1)
            update(e, s_sc[e, 0], j)
        for e in range(heads):
            s_sc[e, 0] = scores(e, j + 2)
            update(e, s_sc[e, 1], j + 1)
        return carry

    lax.fori_loop(0, i // 2, body, 0)

    @pl.when(i % 2 == 0)
    def _():
        for e in range(heads):
            update(e, masked(s_sc[e, 0]), i)

    @pl.when(i % 2 == 1)
    def _():
        for e in range(heads):
            s_sc[e, 1] = scores(e, i)
            update(e, s_sc[e, 0], i - 1)
        for e in range(heads):
            update(e, masked(s_sc[e, 1]), i)

    for e in range(heads):
        o_ref[0, :, V_HEAD_DIM * e:V_HEAD_DIM * (e + 1)] = (
            jnp.transpose(acc_sc[e] / l_sc[e]).astype(o_ref.dtype))


def _attention(q, k, vt):
    b, lp, _ = q.shape
    tile = SEQ_TILE
    hs = ATTN_STEP_HEADS
    return pl.pallas_call(
        functools.partial(_attn_kernel, tile=tile, heads=hs),
        grid=(b, ATTN_HEADS // hs, lp // tile),
        in_specs=[pl.BlockSpec((1, tile, hs * HEAD_PAD), lambda bi, hi, qi: (bi, qi, hi)),
                  pl.BlockSpec((1, lp, hs * HEAD_PAD), lambda bi, hi, qi: (bi, 0, hi)),
                  pl.BlockSpec((1, hs * V_HEAD_DIM, lp), lambda bi, hi, qi: (bi, hi, 0))],
        out_specs=pl.BlockSpec((1, tile, hs * V_HEAD_DIM), lambda bi, hi, qi: (bi, qi, hi)),
        out_shape=jax.ShapeDtypeStruct((b, lp, ATTN_WIDTH), BF16),
        scratch_shapes=[pltpu.VMEM((hs, 2, tile, tile), F32),
                        pltpu.VMEM((hs, 1, tile), F32), pltpu.VMEM((hs, 1, tile), F32),
                        pltpu.VMEM((hs, V_HEAD_DIM, tile), F32)],
        compiler_params=pltpu.CompilerParams(
            dimension_semantics=("arbitrary", "arbitrary", "arbitrary"),
            vmem_limit_bytes=VMEM_LIMIT),
        name="attention",
    )(q, k, vt)


def _regroup(v):
    lane = lax.broadcasted_iota(jnp.int32, v[0].shape, 1)
    first_half = lane < 2 * PAIR
    even_block = (lane // PAIR) % 2 == 0
    swap = lambda x: pltpu.roll(x, 2 * PAIR, 1)
    t0 = jnp.where(first_half, v[0], swap(v[2]))
    t2 = jnp.where(first_half, swap(v[0]), v[2])
    t1 = jnp.where(first_half, v[1], swap(v[3]))
    t3 = jnp.where(first_half, swap(v[1]), v[3])
    up = lambda x: pltpu.roll(x, PAIR, 1)
    down = lambda x: pltpu.roll(x, LANE - PAIR, 1)
    return (jnp.where(even_block, t0, up(t1)), jnp.where(even_block, down(t0), t1),
            jnp.where(even_block, t2, up(t3)), jnp.where(even_block, down(t2), t3))


def _ssm_kernel(u_ref, ws_ref, a_ref, t_ref, wc_ref, d_ref, o_ref, x_sc, v_sc, sin_sc, st_sc,
                *, batch, nsub, pitch):
    nslab = 2 * PAIRS
    half = PAIRS

    @pl.when(pl.program_id(1) == 0)
    def _():
        st_sc[...] = jnp.zeros(st_sc.shape, F32)

    @pl.when((pl.program_id(0) == 0) & (pl.program_id(1) == 0))
    def _():
        x_sc[...] = jnp.zeros(x_sc.shape, F32)
        sin_sc[...] = jnp.zeros(sin_sc.shape, F32)

    for b in range(batch):
        for m in range(SUB // PAIRS):
            pieces = _regroup([u_ref[b, pl.ds(PAIRS * m + i, nsub, stride=SUB), :] for i in range(PAIRS)])
            for j in range(PAIRS):
                x_sc[j, b * pitch:b * pitch + nsub, LANE * m:LANE * (m + 1)] = pieces[j]
    for j in range(PAIRS):
        v = jnp.dot(x_sc[j].astype(BF16), ws_ref[0, j], preferred_element_type=F32)
        v_sc[j] = v[:, :LANE]
        v_sc[half + j] = v[:, LANE:]
    a_re = [jnp.broadcast_to(a_ref[0, 0:1, LANE * j:LANE * (j + 1)], (batch, LANE)) for j in range(half)]
    a_im = [jnp.broadcast_to(a_ref[0, 1:2, LANE * j:LANE * (j + 1)], (batch, LANE)) for j in range(half)]

    def body(r, carry):
        rows = pl.ds(r, batch, stride=pitch)
        out = []
        for j in range(half):
            s_re, s_im = carry[j], carry[half + j]
            sin_sc[j, rows, :] = s_re
            sin_sc[half + j, rows, :] = s_im
            out.append((a_re[j] * s_re - a_im[j] * s_im + v_sc[j, rows, :],
                        a_re[j] * s_im + a_im[j] * s_re + v_sc[half + j, rows, :]))
        return tuple(o[0] for o in out) + tuple(o[1] for o in out)

    state = lax.fori_loop(0, nsub, body, tuple(st_sc[j] for j in range(nslab)), unroll=4)
    for j in range(nslab):
        st_sc[j] = state[j]
    for n in range(X_W // MXU):
        cols = slice(MXU * n, MXU * (n + 1))
        kdim = MXU * (n + 1)
        g = []
        for j in range(PAIRS):
            s_in = jnp.concatenate([sin_sc[j], sin_sc[half + j]], axis=1).astype(BF16)
            y = (jnp.dot(x_sc[j, :, :kdim].astype(BF16), t_ref[0, j, :kdim, cols], preferred_element_type=F32)
                 + jnp.dot(s_in, wc_ref[0, j, :, cols], preferred_element_type=F32)
                 + d_ref[0, j, :, cols] * x_sc[j, :, cols])
            g.append(jax.nn.gelu(y))
        for mm in range(MXU // LANE):
            m = n * (MXU // LANE) + mm
            for b in range(batch):
                pieces = _regroup([gj[b * pitch:b * pitch + nsub, LANE * mm:LANE * (mm + 1)] for gj in g])
                for i in range(PAIRS):
                    o_ref[b, pl.ds(PAIRS * m + i, nsub, stride=SUB), :] = pieces[i]


def _ssm(u, ws, a16, toep, wc, d_x):
    batch, lp, _ = u.shape
    nsub = SSM_TILE_SUB
    rows = batch * SSM_PITCH
    nslab = 2 * PAIRS
    wspec = lambda a: pl.BlockSpec((1,) + a.shape[1:], lambda q, c: (q,) + (0,) * (a.ndim - 1),
                                   pipeline_mode=pl.Buffered(1))
    return pl.pallas_call(
        functools.partial(_ssm_kernel, batch=batch, nsub=nsub, pitch=SSM_PITCH),
        grid=(QUADS, lp // SEQ_TILE),
        in_specs=[pl.BlockSpec((batch, SEQ_TILE, LANE), lambda q, c: (0, c, q)),
                  wspec(ws), wspec(a16), wspec(toep), wspec(wc), wspec(d_x)],
        out_specs=pl.BlockSpec((batch, SEQ_TILE, LANE), lambda q, c: (0, c, q)),
        out_shape=jax.ShapeDtypeStruct((batch, lp, SSM_WIDTH), F32),
        scratch_shapes=[pltpu.VMEM((PAIRS, rows, X_W), F32), pltpu.VMEM((nslab, rows, LANE), F32),
                        pltpu.VMEM((nslab, rows, LANE), F32), pltpu.VMEM((nslab, batch, LANE), F32)],
        compiler_params=pltpu.CompilerParams(dimension_semantics=("arbitrary", "arbitrary"),
                                             vmem_limit_bytes=VMEM_LIMIT),
        name="ssm",
    )(u, ws, a16, toep, wc, d_x)


def _ssm_tables(a_re, a_im, log_dt, b_re, b_im, c_re, c_im, d_skip):
    g, p, hc = SSM_GROUPS, SSM_STATE, SSM_GROUP
    npair = g // 2
    hi = lax.Precision.HIGHEST
    lr = jnp.minimum(a_re.astype(F32), -1e-4)
    li = a_im.astype(F32)
    dt = jnp.exp(log_dt.astype(F32))[:, None]
    mag = jnp.exp(lr * dt)
    lam_re = mag * jnp.cos(li * dt)
    lam_im = mag * jnp.sin(li * dt)
    nr, ni = lam_re - 1.0, lam_im
    den = lr * lr + li * li
    coef_re = (nr * lr + ni * li) / den
    coef_im = (ni * lr - nr * li) / den
    br = jnp.transpose(b_re.astype(F32), (0, 2, 1))
    bi = jnp.transpose(b_im.astype(F32), (0, 2, 1))
    bbar_re = coef_re[:, None, :] * br - coef_im[:, None, :] * bi
    bbar_im = coef_re[:, None, :] * bi + coef_im[:, None, :] * br
    cr, ci = c_re.astype(F32), c_im.astype(F32)
    kk = jnp.arange(SUB + 1, dtype=F32)[None, :, None]
    ph = (li * dt)[:, None, :] * kk
    pmag = jnp.exp((lr * dt)[:, None, :] * kk)
    pw_re = pmag * jnp.cos(ph)
    pw_im = pmag * jnp.sin(ph)

    iota = lambda n: jnp.arange(n)
    row_g = lambda n, per: (iota(n)[:, None] // per) % 2
    col_g = lambda n, per: (iota(n)[None, :] // per) % 2
    kw = SUB * hc
    spread = ((iota(kw)[:, None] // hc == iota(X_W)[None, :] // PAIR)
              & (iota(kw)[:, None] % hc == iota(X_W)[None, :] % hc)).astype(F32)

    cl_re = cr[:, None] * pw_re[:, :SUB, None, :] - ci[:, None] * pw_im[:, :SUB, None, :]
    cl_im = cr[:, None] * pw_im[:, :SUB, None, :] + ci[:, None] * pw_re[:, :SUB, None, :]
    kern = jnp.einsum('ghp,gkop->ghko', jnp.concatenate([bbar_re, -bbar_im], axis=-1),
                      jnp.concatenate([cl_re, cl_im], axis=-1), precision=hi).reshape(g * hc, kw)
    shift = ((iota(kw)[None, :, None] // hc + iota(SUB)[:, None, None] == iota(kw)[None, None, :] // hc)
             & (iota(kw)[None, :, None] % hc == iota(kw)[None, None, :] % hc)).astype(F32)
    toep = jnp.einsum('ra,sac->src', kern, shift, precision=hi)
    toep = jnp.transpose(toep.reshape(SUB, npair, PAIR, kw), (1, 0, 2, 3)).reshape(npair, X_W, kw)
    toep = jnp.einsum('qra,ac->qrc', toep, spread, precision=hi)
    toep = jnp.where(row_g(X_W, hc) == col_g(X_W, hc), toep, 0.0).astype(BF16)

    kr = (SUB - 1) - kk[:, :SUB]
    rph = (li * dt)[:, None, :] * kr
    rmag = jnp.exp((lr * dt)[:, None, :] * kr)
    rv_re = jnp.transpose(rmag * jnp.cos(rph), (1, 0, 2))
    rv_im = jnp.transpose(rmag * jnp.sin(rph), (1, 0, 2))

    def ws_tile(w):
        w = jnp.transpose(w.reshape(SUB, npair, PAIR, p), (1, 0, 2, 3)).reshape(npair, X_W, p)
        w = jnp.tile(w, (1, 1, 2))
        return jnp.where(row_g(X_W, hc) == col_g(ST_W, p), w, 0.0).astype(BF16)

    ws = jnp.concatenate(
        [ws_tile(rv_re[:, :, None, :] * bbar_re[None] - rv_im[:, :, None, :] * bbar_im[None]),
         ws_tile(rv_re[:, :, None, :] * bbar_im[None] + rv_im[:, :, None, :] * bbar_re[None])],
        axis=-1)

    rep_t = (iota(SUB)[:, None] == iota(kw)[None, :] // hc).astype(F32)
    rep_o = (iota(hc)[:, None] == iota(kw)[None, :] % hc).astype(F32)
    c_t = lambda c: jnp.einsum('gpo,oc->gpc', jnp.transpose(c, (0, 2, 1)), rep_o, precision=hi)
    l_t = lambda w: jnp.einsum('gpt,tc->gpc', jnp.transpose(w[:, 1:], (0, 2, 1)), rep_t, precision=hi)
    cr_t, ci_t, lr_t, li_t = c_t(cr), c_t(ci), l_t(pw_re), l_t(pw_im)

    def wc_tile(w):
        w = jnp.einsum('qra,ac->qrc', w.reshape(npair, ST_W, kw), spread, precision=hi)
        return jnp.where(row_g(ST_W, p) == col_g(X_W, hc), w, 0.0).astype(BF16)

    wc = jnp.concatenate([wc_tile(cr_t * lr_t - ci_t * li_t),
                          wc_tile(-(cr_t * li_t + ci_t * lr_t))], axis=1)
    quad = lambda w: w.reshape((QUADS, PAIRS) + w.shape[1:])
    a16 = jnp.stack([pw_re[:, SUB].reshape(QUADS, LANE * PAIRS), pw_im[:, SUB].reshape(QUADS, LANE * PAIRS)],
                    axis=1)
    d_x = jnp.tile(d_skip.astype(F32).reshape(npair, 1, PAIR), (1, 1, SUB))
    return quad(ws), a16, quad(toep), quad(wc), quad(d_x)


def _mix_ffn_kernel(h_ref, attn_ref, g_ref, wglu_ref, ga_ref, gs_ref, wo_ref, gffn_ref,
                    wg_ref, wu_ref, wd_ref, gfin_ref, o_ref, *, final):
    g = g_ref[...]
    ssm = g * jax.nn.sigmoid(jnp.dot(g.astype(BF16), wglu_ref[...], preferred_element_type=F32))
    a_n = _rms(attn_ref[...].astype(F32), ga_ref[...]).astype(BF16)
    s_n = _rms(ssm, gs_ref[...]).astype(BF16)
    h1 = (h_ref[...]
          + jnp.dot(a_n, wo_ref[:ATTN_WIDTH, :], preferred_element_type=F32)
          + jnp.dot(s_n, wo_ref[ATTN_WIDTH:, :], preferred_element_type=F32))
    hn = _rms(h1, gffn_ref[...]).astype(BF16)
    gate = jnp.dot(hn, wg_ref[...], preferred_element_type=F32)
    up = jnp.dot(hn, wu_ref[...], preferred_element_type=F32)
    act = (jax.nn.silu(gate) * up).astype(BF16)
    o_ref[...] = h1
    o_ref[...] += jnp.dot(act, wd_ref[...], preferred_element_type=F32)
    if final:
        o_ref[...] = _rms(o_ref[...], gfin_ref[...])


def _mix_ffn(h, attn, g, wglu, ga, gs, wo, gffn, wg, wu, wd, gfin, *, final, lp, first, count):
    n = h.shape[0]
    row = lambda i: (i, 0)
    const = lambda i: (0, 0)
    full = lambda a: pl.BlockSpec(a.shape, const, pipeline_mode=pl.Buffered(1))
    if final:
        tm = OUT_TILE
        per_seq = count // tm
        n_out = (n // lp) * count
        align = math.gcd(lp, first, tm)
        src = lambda i: (pl.multiple_of((i // per_seq) * lp + first + (i % per_seq) * tm, align), 0)
        in_rows = lambda width: pl.BlockSpec((pl.Element(tm), pl.Element(width)), src)
    else:
        tm = SEQ_TILE
        n_out = n
        in_rows = lambda width: pl.BlockSpec((tm, width), row)
    return pl.pallas_call(
        functools.partial(_mix_ffn_kernel, final=final),
        grid=(n_out // tm,),
        in_specs=[in_rows(D_MODEL), in_rows(ATTN_WIDTH), in_rows(SSM_WIDTH),
                  full(wglu), full(ga), full(gs), full(wo), full(gffn),
                  full(wg), full(wu), full(wd), full(gfin)],
        out_specs=pl.BlockSpec((tm, D_MODEL), row),
        out_shape=jax.ShapeDtypeStruct((n_out, D_MODEL), F32),
        compiler_params=pltpu.CompilerParams(dimension_semantics=("arbitrary",),
                                             vmem_limit_bytes=VMEM_LIMIT),
        name="mix_ffn",
    )(h, attn, g, wglu, ga, gs, wo, gffn, wg, wu, wd, gfin)


def _prep_w_in(w):
    cq = w[:, :Q_LORA_RANK]
    ckv = w[:, Q_LORA_RANK:Q_LORA_RANK + KV_LORA_RANK]
    kr = w[:, Q_LORA_RANK + KV_LORA_RANK:Q_LORA_RANK + KV_LORA_RANK + QK_ROPE_DIM]
    u = w[:, Q_LORA_RANK + KV_LORA_RANK + QK_ROPE_DIM:]
    x1, x2 = kr[:, :QK_ROPE_DIM // 2], kr[:, QK_ROPE_DIM // 2:]
    return jnp.concatenate([cq, ckv, u, x1, x2, x2, x1], axis=1).astype(BF16)


def _prep_w_uq(w):
    w = w.reshape(Q_LORA_RANK, ATTN_HEADS, QK_NOPE_DIM + QK_ROPE_DIM)
    nope = w[..., :QK_NOPE_DIM]
    x1 = w[..., QK_NOPE_DIM:QK_NOPE_DIM + QK_ROPE_DIM // 2]
    x2 = w[..., QK_NOPE_DIM + QK_ROPE_DIM // 2:]
    return jnp.concatenate([nope, x1, x2, x2, x1], axis=-1).reshape(Q_LORA_RANK, ATTN_HEADS * HEAD_PAD).astype(BF16)


def _prep_w_ukv(w):
    w = w.reshape(KV_LORA_RANK, ATTN_HEADS, QK_NOPE_DIM + V_HEAD_DIM)
    k = w[..., :QK_NOPE_DIM].reshape(KV_LORA_RANK, ATTN_HEADS * QK_NOPE_DIM)
    v = w[..., QK_NOPE_DIM:].reshape(KV_LORA_RANK, ATTN_WIDTH)
    return k.astype(BF16), jnp.transpose(v).astype(BF16)


def kernel(x, meta_tokens, norm_mix_g, w_in, q_norm_g, w_uq, kv_norm_g, w_ukv, ssm_a_re, ssm_a_im, ssm_log_dt, ssm_b_re, ssm_b_im, ssm_c_re, ssm_c_im, ssm_d, w_glu, attn_out_g, ssm_out_g, w_o, norm_ffn_g, w_gate, w_up, w_down, final_norm_g):
    b, seq, d = x.shape
    m = NUM_META
    length = seq + m
    lp = -(-length // SEQ_TILE) * SEQ_TILE
    depth = w_in.shape[0]
    h = jnp.concatenate([jnp.broadcast_to(meta_tokens[None].astype(x.dtype), (b, m, d)), x,
                         jnp.zeros((b, lp - length, d), x.dtype)], axis=1).reshape(b * lp, d)
    pos_f = jnp.arange(lp, dtype=F32)
    inv_freq = 1.0 / (ROPE_THETA ** (jnp.arange(0, QK_ROPE_DIM, 2, dtype=F32) / QK_ROPE_DIM))
    ang = pos_f[:, None] * inv_freq[None, :]
    cos, sin = jnp.cos(ang), jnp.sin(ang)
    zpad = jnp.zeros((lp, LANE - QK_ROPE_DIM), F32)
    ctab = jnp.concatenate([cos, cos, zpad], axis=1)
    stab = jnp.concatenate([-sin, sin, zpad], axis=1)
    row2 = lambda v: v.reshape(1, -1).astype(F32)
    for l in range(depth):
        q, k, vt, u = _proj(h, ctab, stab, row2(norm_mix_g[l]), _prep_w_in(w_in[l]),
                            row2(q_norm_g[l]), _prep_w_uq(w_uq[l]), row2(kv_norm_g[l]),
                            *_prep_w_ukv(w_ukv[l]), tiles_per_seq=lp // SEQ_TILE)
        attn = _attention(q.reshape(b, lp, -1), k.reshape(b, lp, -1), vt)
        g = _ssm(u.reshape(b, lp, SSM_WIDTH),
                 *_ssm_tables(ssm_a_re[l], ssm_a_im[l], ssm_log_dt[l], ssm_b_re[l], ssm_b_im[l],
                              ssm_c_re[l], ssm_c_im[l], ssm_d[l]))
        h = _mix_ffn(h, attn.reshape(b * lp, -1), g.reshape(b * lp, SSM_WIDTH),
                     w_glu[l].astype(BF16), row2(attn_out_g[l]), row2(ssm_out_g[l]),
                     w_o[l].astype(BF16), row2(norm_ffn_g[l]), w_gate[l].astype(BF16),
                     w_up[l].astype(BF16), w_down[l].astype(BF16), row2(final_norm_g),
                     final=(l == depth - 1), lp=lp, first=m, count=seq)
    return h.reshape(b, seq, d)
```

```python
import functools
import math

import jax
import jax.numpy as jnp
from jax import lax
from jax.experimental import pallas as pl
from jax.experimental.pallas import tpu as pltpu

F32 = jnp.float32
BF16 = jnp.bfloat16

D_MODEL = 1024
NUM_META = 16
ATTN_HEADS = 4
QK_NOPE_DIM = 128
QK_ROPE_DIM = 64
V_HEAD_DIM = 128
Q_LORA_RANK = 384
KV_LORA_RANK = 256
ATTN_WIDTH = ATTN_HEADS * V_HEAD_DIM
ATTN_SCALE = 1.0 / math.sqrt(QK_NOPE_DIM + QK_ROPE_DIM)
ROPE_THETA = 10000.0
SSM_WIDTH = 512
SSM_GROUP = 16
SSM_GROUPS = SSM_WIDTH // SSM_GROUP
SSM_STATE = 64
FFN_HIDDEN = 2816
RMS_EPS = 1e-6

LANE = 128
MXU = 256
HEAD_PAD = 2 * LANE
SUB = 16
PAIR = 2 * SSM_GROUP
PAIRS = LANE // PAIR
QUADS = SSM_WIDTH // LANE
X_W = SUB * PAIR
ST_W = 2 * SSM_STATE
SEQ_TILE = 768
SSM_TILE = 1408
SSM_TILE_SUB = SSM_TILE // SUB
LOG2E = math.log2(math.e)
ATTN_STEP_HEADS = 2
OUT_TILE = 512
NEG_BIG = -1e30
VMEM_LIMIT = 56 * 1024 * 1024


def _rms(x, g):
    y = x * lax.rsqrt(jnp.mean(x * x, axis=-1, keepdims=True) + RMS_EPS)
    return y * g


def _proj_kernel(h_ref, ct_ref, st_ref, gmix_ref, win_ref, gq_ref, wuq_ref, gkv_ref, wuk_ref, wuvt_ref,
                 q_ref, k_ref, vt_ref, u_ref):
    hn = _rms(h_ref[...], gmix_ref[...]).astype(BF16)
    z = jnp.dot(hn, win_ref[...], preferred_element_type=F32)
    cq = _rms(z[:, :Q_LORA_RANK], gq_ref[...]).astype(BF16)
    c0 = Q_LORA_RANK + KV_LORA_RANK
    ckv = _rms(z[:, Q_LORA_RANK:c0], gkv_ref[...]).astype(BF16)
    u_ref[...] = z[:, c0:c0 + SSM_WIDTH]
    ct = ct_ref[...]
    st = st_ref[...]

    def rope(a):
        return a * ct + pltpu.roll(a, 2 * (QK_ROPE_DIM // 2), 1) * st

    kr = rope(z[:, c0 + SSM_WIDTH:]).astype(BF16)
    q = jnp.dot(cq, wuq_ref[...], preferred_element_type=F32) * (ATTN_SCALE * LOG2E)
    kn = jnp.dot(ckv, wuk_ref[...], preferred_element_type=F32)
    for h in range(ATTN_HEADS):
        lo = HEAD_PAD * h
        q_ref[:, lo:lo + LANE] = q[:, lo:lo + LANE].astype(BF16)
        q_ref[:, lo + LANE:lo + HEAD_PAD] = rope(q[:, lo + LANE:lo + HEAD_PAD]).astype(BF16)
        k_ref[:, lo:lo + LANE] = kn[:, LANE * h:LANE * (h + 1)].astype(BF16)
        k_ref[:, lo + LANE:lo + HEAD_PAD] = kr
    vt_ref[0] = lax.dot_general(wuvt_ref[...], ckv, (((1,), (1,)), ((), ())),
                                preferred_element_type=F32).astype(BF16)


def _layer_spec(a, layer, **kw):
    index = lambda *_: (layer,) + (0,) * (a.ndim - 1)
    return pl.BlockSpec((None,) + a.shape[1:], index, **kw)


def _proj(h, ctab, stab, gmix, win, gq, wuq, gkv, wuk, wuvt, *, layer, tiles_per_seq):
    n = h.shape[0]
    tm = SEQ_TILE
    row = lambda i: (i, 0)
    pos = lambda i: (i % tiles_per_seq, 0)
    full = lambda a: _layer_spec(a, layer)
    return pl.pallas_call(
        _proj_kernel,
        grid=(n // tm,),
        in_specs=[pl.BlockSpec((tm, D_MODEL), row),
                  pl.BlockSpec((tm, LANE), pos), pl.BlockSpec((tm, LANE), pos),
                  full(gmix), full(win), full(gq), full(wuq), full(gkv), full(wuk), full(wuvt)],
        out_specs=[pl.BlockSpec((tm, ATTN_HEADS * HEAD_PAD), row),
                   pl.BlockSpec((tm, ATTN_HEADS * HEAD_PAD), row),
                   pl.BlockSpec((1, ATTN_WIDTH, tm), lambda i: (i // tiles_per_seq, 0, i % tiles_per_seq)),
                   pl.BlockSpec((tm, SSM_WIDTH), row)],
        out_shape=[jax.ShapeDtypeStruct((n, ATTN_HEADS * HEAD_PAD), BF16),
                   jax.ShapeDtypeStruct((n, ATTN_HEADS * HEAD_PAD), BF16),
                   jax.ShapeDtypeStruct((n // (tm * tiles_per_seq), ATTN_WIDTH, tm * tiles_per_seq), BF16),
                   jax.ShapeDtypeStruct((n, SSM_WIDTH), F32)],
        compiler_params=pltpu.CompilerParams(dimension_semantics=("arbitrary",),
                                             vmem_limit_bytes=VMEM_LIMIT),
        name="proj",
    )(h, ctab, stab, gmix, win, gq, wuq, gkv, wuk, wuvt)


def _attn_kernel(q_ref, k_ref, vt_ref, o_ref, s_sc, smax_sc, m_sc, l_sc, acc_sc, *, tile, heads):
    i = pl.program_id(2)
    m_sc[...] = jnp.full(m_sc.shape, NEG_BIG, F32)
    l_sc[...] = jnp.zeros(l_sc.shape, F32)
    acc_sc[...] = jnp.zeros(acc_sc.shape, F32)

    def scores(e, j, slot):
        k = k_ref[0, pl.ds(pl.multiple_of(j * tile, tile), tile), HEAD_PAD * e:HEAD_PAD * (e + 1)]
        q = q_ref[0, :, HEAD_PAD * e:HEAD_PAD * (e + 1)]
        s = lax.dot_general(k, q, (((1,), (1,)), ((), ())), preferred_element_type=F32)
        s_sc[e, slot] = s
        smax_sc[e, slot] = jnp.max(s, axis=0, keepdims=True)

    def update(e, s, smax, j):
        vt = vt_ref[0, V_HEAD_DIM * e:V_HEAD_DIM * (e + 1), pl.ds(pl.multiple_of(j * tile, tile), tile)]
        m_prev = m_sc[e]
        m_new = jnp.maximum(m_prev, smax)
        alpha = jnp.exp2(m_prev - m_new)
        p = jnp.exp2(s - m_new)
        l_sc[e] = alpha * l_sc[e] + jnp.sum(p, axis=0, keepdims=True)
        acc_sc[e] = alpha * acc_sc[e] + jnp.dot(vt, p.astype(BF16), preferred_element_type=F32)
        m_sc[e] = m_new

    def update_full(e, slot, j):
        update(e, s_sc[e, slot], smax_sc[e, slot], j)

    def update_diagonal(e, slot, j):
        s = masked(s_sc[e, slot])
        update(e, s, jnp.max(s, axis=0, keepdims=True), j)

    def masked(s):
        keys = lax.broadcasted_iota(jnp.int32, s.shape, 0)
        queries = lax.broadcasted_iota(jnp.int32, s.shape, 1)
        return jnp.where(keys <= queries, s, NEG_BIG)

    for e in range(heads):
        scores(e, 0, 0)

    def body(t, carry):
        j = 2 * t
        for e in range(heads):
            scores(e, j + 1, 1)
            update_full(e, 0, j)
        for e in range(heads):
            scores(e, j + 2, 0)
            update_full(e, 1, j + 1)
        return carry

    lax.fori_loop(0, i // 2, body, 0)

    @pl.when(i % 2 == 0)
    def _():
        for e in range(heads):
            update_diagonal(e, 0, i)

    @pl.when(i % 2 == 1)
    def _():
        for e in range(heads):
            scores(e, i, 1)
            update_full(e, 0, i - 1)
        for e in range(heads):
            update_diagonal(e, 1, i)

    for e in range(heads):
        o_ref[0, :, V_HEAD_DIM * e:V_HEAD_DIM * (e + 1)] = (
            jnp.transpose(acc_sc[e] / l_sc[e]).astype(o_ref.dtype))


def _attention(q, k, vt):
    b, lp, _ = q.shape
    tile = SEQ_TILE
    hs = ATTN_STEP_HEADS
    return pl.pallas_call(
        functools.partial(_attn_kernel, tile=tile, heads=hs),
        grid=(b, ATTN_HEADS // hs, lp // tile),
        in_specs=[pl.BlockSpec((1, tile, hs * HEAD_PAD), lambda bi, hi, qi: (bi, qi, hi)),
                  pl.BlockSpec((1, lp, hs * HEAD_PAD), lambda bi, hi, qi: (bi, 0, hi)),
                  pl.BlockSpec((1, hs * V_HEAD_DIM, lp), lambda bi, hi, qi: (bi, hi, 0))],
        out_specs=pl.BlockSpec((1, tile, hs * V_HEAD_DIM), lambda bi, hi, qi: (bi, qi, hi)),
        out_shape=jax.ShapeDtypeStruct((b, lp, ATTN_WIDTH), BF16),
        scratch_shapes=[pltpu.VMEM((hs, 2, tile, tile), F32), pltpu.VMEM((hs, 2, 1, tile), F32),
                        pltpu.VMEM((hs, 1, tile), F32), pltpu.VMEM((hs, 1, tile), F32),
                        pltpu.VMEM((hs, V_HEAD_DIM, tile), F32)],
        compiler_params=pltpu.CompilerParams(
            dimension_semantics=("arbitrary", "arbitrary", "arbitrary"),
            vmem_limit_bytes=VMEM_LIMIT),
        name="attention",
    )(q, k, vt)


def _regroup(v):
    lane = lax.broadcasted_iota(jnp.int32, v[0].shape, 1)
    first_half = lane < 2 * PAIR
    even_block = (lane // PAIR) % 2 == 0
    swap = lambda x: pltpu.roll(x, 2 * PAIR, 1)
    t0 = jnp.where(first_half, v[0], swap(v[2]))
    t2 = jnp.where(first_half, swap(v[0]), v[2])
    t1 = jnp.where(first_half, v[1], swap(v[3]))
    t3 = jnp.where(first_half, swap(v[1]), v[3])
    up = lambda x: pltpu.roll(x, PAIR, 1)
    down = lambda x: pltpu.roll(x, LANE - PAIR, 1)
    return (jnp.where(even_block, t0, up(t1)), jnp.where(even_block, down(t0), t1),
            jnp.where(even_block, t2, up(t3)), jnp.where(even_block, down(t2), t3))


def _ssm_kernel(u_ref, ws_ref, a_ref, t_ref, wc_ref, d_ref, o_ref, x_sc, v_sc, sin_sc, st_sc,
                *, batch, nsub):
    nslab = 2 * PAIRS
    half = PAIRS

    @pl.when(pl.program_id(1) == 0)
    def _():
        st_sc[...] = jnp.zeros(st_sc.shape, F32)

    for b in range(batch):
        for m in range(SUB // PAIRS):
            pieces = _regroup([u_ref[b, pl.ds(PAIRS * m + i, nsub, stride=SUB), :] for i in range(PAIRS)])
            for j in range(PAIRS):
                x_sc[j, b * nsub:(b + 1) * nsub, LANE * m:LANE * (m + 1)] = pieces[j]
    for j in range(PAIRS):
        v = jnp.dot(x_sc[j].astype(BF16), ws_ref[0, j], preferred_element_type=F32)
        v_sc[j] = v[:, :LANE]
        v_sc[half + j] = v[:, LANE:]
    a_re = [jnp.broadcast_to(a_ref[0, 0:1, LANE * j:LANE * (j + 1)], (batch, LANE)) for j in range(half)]
    a_im = [jnp.broadcast_to(a_ref[0, 1:2, LANE * j:LANE * (j + 1)], (batch, LANE)) for j in range(half)]

    def body(r, carry):
        rows = pl.ds(r, batch, stride=nsub)
        out = []
        for j in range(half):
            s_re, s_im = carry[j], carry[half + j]
            sin_sc[j, rows, :] = s_re
            sin_sc[half + j, rows, :] = s_im
            out.append((a_re[j] * s_re - a_im[j] * s_im + v_sc[j, rows, :],
                        a_re[j] * s_im + a_im[j] * s_re + v_sc[half + j, rows, :]))
        return tuple(o[0] for o in out) + tuple(o[1] for o in out)

    state = lax.fori_loop(0, nsub, body, tuple(st_sc[j] for j in range(nslab)), unroll=4)
    for j in range(nslab):
        st_sc[j] = state[j]
    for n in range(X_W // MXU):
        cols = slice(MXU * n, MXU * (n + 1))
        kdim = MXU * (n + 1)
        g = []
        for j in range(PAIRS):
            s_in = jnp.concatenate([sin_sc[j], sin_sc[half + j]], axis=1).astype(BF16)
            y = (jnp.dot(x_sc[j, :, :kdim].astype(BF16), t_ref[0, j, :kdim, cols], preferred_element_type=F32)
                 + jnp.dot(s_in, wc_ref[0, j, :, cols], preferred_element_type=F32)
                 + d_ref[0, j, :, cols] * x_sc[j, :, cols])
            g.append(jax.nn.gelu(y))
        for mm in range(MXU // LANE):
            m = n * (MXU // LANE) + mm
            for b in range(batch):
                pieces = _regroup([gj[b * nsub:(b + 1) * nsub, LANE * mm:LANE * (mm + 1)] for gj in g])
                for i in range(PAIRS):
                    o_ref[b, pl.ds(PAIRS * m + i, nsub, stride=SUB), :] = pieces[i]


def _ssm(u, ws, a16, toep, wc, d_x, *, layer):
    batch, lp, _ = u.shape
    nsub = SSM_TILE_SUB
    rows = batch * nsub
    nslab = 2 * PAIRS
    wspec = lambda a: pl.BlockSpec((None, 1) + a.shape[2:], lambda q, c: (layer, q) + (0,) * (a.ndim - 2),
                                   pipeline_mode=pl.Buffered(1))
    return pl.pallas_call(
        functools.partial(_ssm_kernel, batch=batch, nsub=nsub),
        grid=(QUADS, lp // SSM_TILE),
        in_specs=[pl.BlockSpec((batch, SSM_TILE, LANE), lambda q, c: (0, c, q)),
                  wspec(ws), wspec(a16), wspec(toep), wspec(wc), wspec(d_x)],
        out_specs=pl.BlockSpec((batch, SSM_TILE, LANE), lambda q, c: (0, c, q)),
        out_shape=jax.ShapeDtypeStruct((batch, lp, SSM_WIDTH), F32),
        scratch_shapes=[pltpu.VMEM((PAIRS, rows, X_W), F32), pltpu.VMEM((nslab, rows, LANE), F32),
                        pltpu.VMEM((nslab, rows, LANE), F32), pltpu.VMEM((nslab, batch, LANE), F32)],
        compiler_params=pltpu.CompilerParams(dimension_semantics=("arbitrary", "arbitrary"),
                                             vmem_limit_bytes=VMEM_LIMIT),
        name="ssm",
    )(u, ws, a16, toep, wc, d_x)


def _ssm_tables(a_re, a_im, log_dt, b_re, b_im, c_re, c_im, d_skip):
    g, p, hc = SSM_GROUPS, SSM_STATE, SSM_GROUP
    npair = g // 2
    hi = lax.Precision.HIGHEST
    lr = jnp.minimum(a_re.astype(F32), -1e-4)
    li = a_im.astype(F32)
    dt = jnp.exp(log_dt.astype(F32))[:, None]
    mag = jnp.exp(lr * dt)
    lam_re = mag * jnp.cos(li * dt)
    lam_im = mag * jnp.sin(li * dt)
    nr, ni = lam_re - 1.0, lam_im
    den = lr * lr + li * li
    coef_re = (nr * lr + ni * li) / den
    coef_im = (ni * lr - nr * li) / den
    br = jnp.transpose(b_re.astype(F32), (0, 2, 1))
    bi = jnp.transpose(b_im.astype(F32), (0, 2, 1))
    bbar_re = coef_re[:, None, :] * br - coef_im[:, None, :] * bi
    bbar_im = coef_re[:, None, :] * bi + coef_im[:, None, :] * br
    cr, ci = c_re.astype(F32), c_im.astype(F32)
    kk = jnp.arange(SUB + 1, dtype=F32)[None, :, None]
    ph = (li * dt)[:, None, :] * kk
    pmag = jnp.exp((lr * dt)[:, None, :] * kk)
    pw_re = pmag * jnp.cos(ph)
    pw_im = pmag * jnp.sin(ph)

    iota = lambda n: jnp.arange(n)
    row_g = lambda n, per: (iota(n)[:, None] // per) % 2
    col_g = lambda n, per: (iota(n)[None, :] // per) % 2
    kw = SUB * hc
    spread = ((iota(kw)[:, None] // hc == iota(X_W)[None, :] // PAIR)
              & (iota(kw)[:, None] % hc == iota(X_W)[None, :] % hc)).astype(F32)

    cl_re = cr[:, None] * pw_re[:, :SUB, None, :] - ci[:, None] * pw_im[:, :SUB, None, :]
    cl_im = cr[:, None] * pw_im[:, :SUB, None, :] + ci[:, None] * pw_re[:, :SUB, None, :]
    kern = jnp.einsum('ghp,gkop->ghko', jnp.concatenate([bbar_re, -bbar_im], axis=-1),
                      jnp.concatenate([cl_re, cl_im], axis=-1), precision=hi).reshape(g * hc, kw)
    shift = ((iota(kw)[None, :, None] // hc + iota(SUB)[:, None, None] == iota(kw)[None, None, :] // hc)
             & (iota(kw)[None, :, None] % hc == iota(kw)[None, None, :] % hc)).astype(F32)
    toep = jnp.einsum('ra,sac->src', kern, shift, precision=hi)
    toep = jnp.transpose(toep.reshape(SUB, npair, PAIR, kw), (1, 0, 2, 3)).reshape(npair, X_W, kw)
    toep = jnp.einsum('qra,ac->qrc', toep, spread, precision=hi)
    toep = jnp.where(row_g(X_W, hc) == col_g(X_W, hc), toep, 0.0).astype(BF16)

    kr = (SUB - 1) - kk[:, :SUB]
    rph = (li * dt)[:, None, :] * kr
    rmag = jnp.exp((lr * dt)[:, None, :] * kr)
    rv_re = jnp.transpose(rmag * jnp.cos(rph), (1, 0, 2))
    rv_im = jnp.transpose(rmag * jnp.sin(rph), (1, 0, 2))

    def ws_tile(w):
        w = jnp.transpose(w.reshape(SUB, npair, PAIR, p), (1, 0, 2, 3)).reshape(npair, X_W, p)
        w = jnp.tile(w, (1, 1, 2))
        return jnp.where(row_g(X_W, hc) == col_g(ST_W, p), w, 0.0).astype(BF16)

    ws = jnp.concatenate(
        [ws_tile(rv_re[:, :, None, :] * bbar_re[None] - rv_im[:, :, None, :] * bbar_im[None]),
         ws_tile(rv_re[:, :, None, :] * bbar_im[None] + rv_im[:, :, None, :] * bbar_re[None])],
        axis=-1)

    rep_t = (iota(SUB)[:, None] == iota(kw)[None, :] // hc).astype(F32)
    rep_o = (iota(hc)[:, None] == iota(kw)[None, :] % hc).astype(F32)
    c_t = lambda c: jnp.einsum('gpo,oc->gpc', jnp.transpose(c, (0, 2, 1)), rep_o, precision=hi)
    l_t = lambda w: jnp.einsum('gpt,tc->gpc', jnp.transpose(w[:, 1:], (0, 2, 1)), rep_t, precision=hi)
    cr_t, ci_t, lr_t, li_t = c_t(cr), c_t(ci), l_t(pw_re), l_t(pw_im)

    def wc_tile(w):
        w = jnp.einsum('qra,ac->qrc', w.reshape(npair, ST_W, kw), spread, precision=hi)
        return jnp.where(row_g(ST_W, p) == col_g(X_W, hc), w, 0.0).astype(BF16)

    wc = jnp.concatenate([wc_tile(cr_t * lr_t - ci_t * li_t),
                          wc_tile(-(cr_t * li_t + ci_t * lr_t))], axis=1)
    quad = lambda w: w.reshape((QUADS, PAIRS) + w.shape[1:])
    a16 = jnp.stack([pw_re[:, SUB].reshape(QUADS, LANE * PAIRS), pw_im[:, SUB].reshape(QUADS, LANE * PAIRS)],
                    axis=1)
    d_x = jnp.tile(d_skip.astype(F32).reshape(npair, 1, PAIR), (1, 1, SUB))
    return quad(ws), a16, quad(toep), quad(wc), quad(d_x)


def _mix_ffn_kernel(h_ref, attn_ref, g_ref, wglu_ref, ga_ref, gs_ref, wo_ref, gffn_ref,
                    wg_ref, wu_ref, wd_ref, gfin_ref, o_ref, *, final):
    g = g_ref[...]
    ssm = g * jax.nn.sigmoid(jnp.dot(g.astype(BF16), wglu_ref[...], preferred_element_type=F32))
    a_n = _rms(attn_ref[...].astype(F32), ga_ref[...]).astype(BF16)
    s_n = _rms(ssm, gs_ref[...]).astype(BF16)
    h1 = (h_ref[...]
          + jnp.dot(a_n, wo_ref[:ATTN_WIDTH, :], preferred_element_type=F32)
          + jnp.dot(s_n, wo_ref[ATTN_WIDTH:, :], preferred_element_type=F32))
    hn = _rms(h1, gffn_ref[...]).astype(BF16)
    gate = jnp.dot(hn, wg_ref[...], preferred_element_type=F32)
    up = jnp.dot(hn, wu_ref[...], preferred_element_type=F32)
    act = (jax.nn.silu(gate) * up).astype(BF16)
    o_ref[...] = h1
    o_ref[...] += jnp.dot(act, wd_ref[...], preferred_element_type=F32)
    if final:
        o_ref[...] = _rms(o_ref[...], gfin_ref[...])


def _mix_ffn(h, attn, g, wglu, ga, gs, wo, gffn, wg, wu, wd, gfin, *, layer, final, lp, first, count):
    n = h.shape[0]
    row = lambda i: (i, 0)
    full = lambda a: _layer_spec(a, layer, pipeline_mode=pl.Buffered(1))
    if final:
        tm = OUT_TILE
        per_seq = count // tm
        n_out = (n // lp) * count
        align = math.gcd(lp, first, tm)
        src = lambda i: (pl.multiple_of((i // per_seq) * lp + first + (i % per_seq) * tm, align), 0)
        in_rows = lambda width: pl.BlockSpec((pl.Element(tm), pl.Element(width)), src)
    else:
        tm = SEQ_TILE
        n_out = n
        in_rows = lambda width: pl.BlockSpec((tm, width), row)
    return pl.pallas_call(
        functools.partial(_mix_ffn_kernel, final=final),
        grid=(n_out // tm,),
        in_specs=[in_rows(D_MODEL), in_rows(ATTN_WIDTH), in_rows(SSM_WIDTH),
                  full(wglu), full(ga), full(gs), full(wo), full(gffn),
                  full(wg), full(wu), full(wd),
                  pl.BlockSpec(gfin.shape, lambda i: (0, 0), pipeline_mode=pl.Buffered(1))],
        out_specs=pl.BlockSpec((tm, D_MODEL), row),
        out_shape=jax.ShapeDtypeStruct((n_out, D_MODEL), F32),
        compiler_params=pltpu.CompilerParams(dimension_semantics=("arbitrary",),
                                             vmem_limit_bytes=VMEM_LIMIT),
        name="mix_ffn",
    )(h, attn, g, wglu, ga, gs, wo, gffn, wg, wu, wd, gfin)


def _prep_w_in(w):
    cq = w[:, :Q_LORA_RANK]
    ckv = w[:, Q_LORA_RANK:Q_LORA_RANK + KV_LORA_RANK]
    kr = w[:, Q_LORA_RANK + KV_LORA_RANK:Q_LORA_RANK + KV_LORA_RANK + QK_ROPE_DIM]
    u = w[:, Q_LORA_RANK + KV_LORA_RANK + QK_ROPE_DIM:]
    x1, x2 = kr[:, :QK_ROPE_DIM // 2], kr[:, QK_ROPE_DIM // 2:]
    return jnp.concatenate([cq, ckv, u, x1, x2, x2, x1], axis=1).astype(BF16)


def _prep_w_uq(w):
    w = w.reshape(Q_LORA_RANK, ATTN_HEADS, QK_NOPE_DIM + QK_ROPE_DIM)
    nope = w[..., :QK_NOPE_DIM]
    x1 = w[..., QK_NOPE_DIM:QK_NOPE_DIM + QK_ROPE_DIM // 2]
    x2 = w[..., QK_NOPE_DIM + QK_ROPE_DIM // 2:]
    return jnp.concatenate([nope, x1, x2, x2, x1], axis=-1).reshape(Q_LORA_RANK, ATTN_HEADS * HEAD_PAD).astype(BF16)


def _prep_w_ukv(w):
    w = w.reshape(KV_LORA_RANK, ATTN_HEADS, QK_NOPE_DIM + V_HEAD_DIM)
    k = w[..., :QK_NOPE_DIM].reshape(KV_LORA_RANK, ATTN_HEADS * QK_NOPE_DIM)
    v = w[..., QK_NOPE_DIM:].reshape(KV_LORA_RANK, ATTN_WIDTH)
    return k.astype(BF16), jnp.transpose(v).astype(BF16)


def kernel(x, meta_tokens, norm_mix_g, w_in, q_norm_g, w_uq, kv_norm_g, w_ukv, ssm_a_re, ssm_a_im, ssm_log_dt, ssm_b_re, ssm_b_im, ssm_c_re, ssm_c_im, ssm_d, w_glu, attn_out_g, ssm_out_g, w_o, norm_ffn_g, w_gate, w_up, w_down, final_norm_g):
    b, seq, d = x.shape
    m = NUM_META
    length = seq + m
    lp = -(-length // SEQ_TILE) * SEQ_TILE
    depth = w_in.shape[0]
    h = jnp.concatenate([jnp.broadcast_to(meta_tokens[None].astype(x.dtype), (b, m, d)), x,
                         jnp.zeros((b, lp - length, d), x.dtype)], axis=1).reshape(b * lp, d)
    pos_f = jnp.arange(lp, dtype=F32)
    inv_freq = 1.0 / (ROPE_THETA ** (jnp.arange(0, QK_ROPE_DIM, 2, dtype=F32) / QK_ROPE_DIM))
    ang = pos_f[:, None] * inv_freq[None, :]
    cos, sin = jnp.cos(ang), jnp.sin(ang)
    zpad = jnp.zeros((lp, LANE - QK_ROPE_DIM), F32)
    ctab = jnp.concatenate([cos, cos, zpad], axis=1)
    stab = jnp.concatenate([-sin, sin, zpad], axis=1)
    gain = lambda v: v.astype(F32)[:, None, :]
    gmix, gq, gkv, ga, gs, gffn = map(gain, (norm_mix_g, q_norm_g, kv_norm_g, attn_out_g, ssm_out_g, norm_ffn_g))
    win, wuq = jax.vmap(_prep_w_in)(w_in), jax.vmap(_prep_w_uq)(w_uq)
    wuk, wuvt = jax.vmap(_prep_w_ukv)(w_ukv)
    tables = jax.vmap(_ssm_tables)(ssm_a_re, ssm_a_im, ssm_log_dt, ssm_b_re, ssm_b_im, ssm_c_re, ssm_c_im, ssm_d)
    wglu, wo, wg, wu, wd = (w.astype(BF16) for w in (w_glu, w_o, w_gate, w_up, w_down))
    gfin = final_norm_g.reshape(1, -1).astype(F32)
    for l in range(depth):
        q, k, vt, u = _proj(h, ctab, stab, gmix, win, gq, wuq, gkv, wuk, wuvt,
                            layer=l, tiles_per_seq=lp // SEQ_TILE)
        attn = _attention(q.reshape(b, lp, -1), k.reshape(b, lp, -1), vt)
        g = _ssm(u.reshape(b, lp, SSM_WIDTH), *tables, layer=l)
        h = _mix_ffn(h, attn.reshape(b * lp, -1), g.reshape(b * lp, SSM_WIDTH),
                     wglu, ga, gs, wo, gffn, wg, wu, wd, gfin,
                     layer=l, final=(l == depth - 1), lp=lp, first=m, count=seq)
    return h.reshape(b, seq, d)
```

```python
import functools
import math

import jax
import jax.numpy as jnp
from jax import lax
from jax.experimental import pallas as pl
from jax.experimental.pallas import tpu as pltpu

F32 = jnp.float32
BF16 = jnp.bfloat16

D_MODEL = 1024
NUM_META = 16
ATTN_HEADS = 4
QK_NOPE_DIM = 128
QK_ROPE_DIM = 64
V_HEAD_DIM = 128
Q_LORA_RANK = 384
KV_LORA_RANK = 256
ATTN_WIDTH = ATTN_HEADS * V_HEAD_DIM
ATTN_SCALE = 1.0 / math.sqrt(QK_NOPE_DIM + QK_ROPE_DIM)
ROPE_THETA = 10000.0
SSM_WIDTH = 512
SSM_GROUP = 16
SSM_GROUPS = SSM_WIDTH // SSM_GROUP
SSM_STATE = 64
FFN_HIDDEN = 2816
RMS_EPS = 1e-6

LANE = 128
MXU = 256
HEAD_PAD = 2 * LANE
SUB = 16
PAIR = 2 * SSM_GROUP
PAIRS = LANE // PAIR
QUADS = SSM_WIDTH // LANE
X_W = SUB * PAIR
ST_W = 2 * SSM_STATE
SEQ_TILE = 768
SSM_TILE = 1408
SSM_TILE_SUB = SSM_TILE // SUB
LOG2E = math.log2(math.e)
ATTN_STEP_HEADS = 2
OUT_TILE = 512
NEG_BIG = -1e30
VMEM_LIMIT = 56 * 1024 * 1024


def _rms(x, g):
    y = x * lax.rsqrt(jnp.mean(x * x, axis=-1, keepdims=True) + RMS_EPS)
    return y * g


def _first_rows(x_ref, meta_ref, h_sc, tile_in_seq, *, meta_rows, seq_rows):
    tm = h_sc.shape[0]

    @pl.when(tile_in_seq == 0)
    def _():
        h_sc[0:meta_rows, :] = meta_ref[...]
        h_sc[meta_rows:, :] = x_ref[0, 0:tm - meta_rows, :]

    @pl.when(tile_in_seq > 0)
    def _():
        x_row = tile_in_seq * tm - meta_rows + lax.broadcasted_iota(jnp.int32, h_sc.shape, 0)
        h_sc[...] = jnp.where(x_row < seq_rows, x_ref[0], 0.0)

    return h_sc[...]


def _proj_kernel(*refs, first, tiles_per_seq, meta_rows, seq_rows):
    if first:
        (x_ref, meta_ref, ct_ref, st_ref, gmix_ref, win_ref, gq_ref, wuq_ref, gkv_ref, wuk_ref, wuvt_ref,
         q_ref, k_ref, vt_ref, u_ref, h_sc) = refs
        h = _first_rows(x_ref, meta_ref, h_sc, pl.program_id(0) % tiles_per_seq,
                        meta_rows=meta_rows, seq_rows=seq_rows)
    else:
        (h_ref, ct_ref, st_ref, gmix_ref, win_ref, gq_ref, wuq_ref, gkv_ref, wuk_ref, wuvt_ref,
         q_ref, k_ref, vt_ref, u_ref) = refs
        h = h_ref[...]
    hn = _rms(h, gmix_ref[...]).astype(BF16)
    z = jnp.dot(hn, win_ref[...], preferred_element_type=F32)
    cq = _rms(z[:, :Q_LORA_RANK], gq_ref[...]).astype(BF16)
    c0 = Q_LORA_RANK + KV_LORA_RANK
    ckv = _rms(z[:, Q_LORA_RANK:c0], gkv_ref[...]).astype(BF16)
    u_ref[...] = z[:, c0:c0 + SSM_WIDTH]
    ct = ct_ref[...]
    st = st_ref[...]

    def rope(a):
        return a * ct + pltpu.roll(a, 2 * (QK_ROPE_DIM // 2), 1) * st

    kr = rope(z[:, c0 + SSM_WIDTH:]).astype(BF16)
    q = jnp.dot(cq, wuq_ref[...], preferred_element_type=F32) * (ATTN_SCALE * LOG2E)
    kn = jnp.dot(ckv, wuk_ref[...], preferred_element_type=F32)
    for h in range(ATTN_HEADS):
        lo = HEAD_PAD * h
        q_ref[:, lo:lo + LANE] = q[:, lo:lo + LANE].astype(BF16)
        q_ref[:, lo + LANE:lo + HEAD_PAD] = rope(q[:, lo + LANE:lo + HEAD_PAD]).astype(BF16)
        k_ref[:, lo:lo + LANE] = kn[:, LANE * h:LANE * (h + 1)].astype(BF16)
        k_ref[:, lo + LANE:lo + HEAD_PAD] = kr
    vt_ref[0] = lax.dot_general(wuvt_ref[...], ckv, (((1,), (1,)), ((), ())),
                                preferred_element_type=F32).astype(BF16)


def _layer_spec(a, layer, **kw):
    index = lambda *_: (layer,) + (0,) * (a.ndim - 1)
    return pl.BlockSpec((None,) + a.shape[1:], index, **kw)


def _x_window(tm, tiles_per_seq, meta_rows, seq_rows):
    overhang = tiles_per_seq * tm - meta_rows - seq_rows
    start = lambda i: pl.multiple_of(jnp.maximum((i % tiles_per_seq) * tm - meta_rows, 0), meta_rows)
    return pl.BlockSpec((pl.Element(1), pl.Element(tm, padding=(0, overhang)), pl.Element(D_MODEL)),
                        lambda i: (i // tiles_per_seq, start(i), 0))


def _proj(h, meta, ctab, stab, gmix, win, gq, wuq, gkv, wuk, wuvt, *, layer, tiles_per_seq, n):
    tm = SEQ_TILE
    first = meta is not None
    row = lambda i: (i, 0)
    pos = lambda i: (i % tiles_per_seq, 0)
    full = lambda a: _layer_spec(a, layer)
    if first:
        meta_rows, seq_rows = meta.shape[0], h.shape[1]
        stream = [_x_window(tm, tiles_per_seq, meta_rows, seq_rows), pl.BlockSpec(meta.shape, lambda i: (0, 0))]
        operands, scratch = (h, meta), [pltpu.VMEM((tm, D_MODEL), F32)]
    else:
        meta_rows = seq_rows = 0
        stream, operands, scratch = [pl.BlockSpec((tm, D_MODEL), row)], (h,), []
    return pl.pallas_call(
        functools.partial(_proj_kernel, first=first, tiles_per_seq=tiles_per_seq,
                          meta_rows=meta_rows, seq_rows=seq_rows),
        grid=(n // tm,),
        scratch_shapes=scratch,
        in_specs=stream + [
                  pl.BlockSpec((tm, LANE), pos), pl.BlockSpec((tm, LANE), pos),
                  full(gmix), full(win), full(gq), full(wuq), full(gkv), full(wuk), full(wuvt)],
        out_specs=[pl.BlockSpec((tm, ATTN_HEADS * HEAD_PAD), row),
                   pl.BlockSpec((tm, ATTN_HEADS * HEAD_PAD), row),
                   pl.BlockSpec((1, ATTN_WIDTH, tm), lambda i: (i // tiles_per_seq, 0, i % tiles_per_seq)),
                   pl.BlockSpec((tm, SSM_WIDTH), row)],
        out_shape=[jax.ShapeDtypeStruct((n, ATTN_HEADS * HEAD_PAD), BF16),
                   jax.ShapeDtypeStruct((n, ATTN_HEADS * HEAD_PAD), BF16),
                   jax.ShapeDtypeStruct((n // (tm * tiles_per_seq), ATTN_WIDTH, tm * tiles_per_seq), BF16),
                   jax.ShapeDtypeStruct((n, SSM_WIDTH), F32)],
        compiler_params=pltpu.CompilerParams(dimension_semantics=("arbitrary",),
                                             vmem_limit_bytes=VMEM_LIMIT),
        name="proj",
    )(*operands, ctab, stab, gmix, win, gq, wuq, gkv, wuk, wuvt)


def _attn_kernel(q_ref, k_ref, vt_ref, o_ref, s_sc, smax_sc, m_sc, l_sc, acc_sc, *, tile, heads):
    i = pl.program_id(2)
    m_sc[...] = jnp.full(m_sc.shape, NEG_BIG, F32)
    l_sc[...] = jnp.zeros(l_sc.shape, F32)
    acc_sc[...] = jnp.zeros(acc_sc.shape, F32)

    def scores(e, j, slot):
        k = k_ref[0, pl.ds(pl.multiple_of(j * tile, tile), tile), HEAD_PAD * e:HEAD_PAD * (e + 1)]
        q = q_ref[0, :, HEAD_PAD * e:HEAD_PAD * (e + 1)]
        s = lax.dot_general(k, q, (((1,), (1,)), ((), ())), preferred_element_type=F32)
        s_sc[e, slot] = s
        smax_sc[e, slot] = jnp.max(s, axis=0, keepdims=True)

    def update(e, s, smax, j):
        vt = vt_ref[0, V_HEAD_DIM * e:V_HEAD_DIM * (e + 1), pl.ds(pl.multiple_of(j * tile, tile), tile)]
        m_prev = m_sc[e]
        m_new = jnp.maximum(m_prev, smax)
        alpha = jnp.exp2(m_prev - m_new)
        p = jnp.exp2(s - m_new)
        l_sc[e] = alpha * l_sc[e] + jnp.sum(p, axis=0, keepdims=True)
        acc_sc[e] = alpha * acc_sc[e] + jnp.dot(vt, p.astype(BF16), preferred_element_type=F32)
        m_sc[e] = m_new

    def update_full(e, slot, j):
        update(e, s_sc[e, slot], smax_sc[e, slot], j)

    def update_diagonal(e, slot, j):
        s = masked(s_sc[e, slot])
        update(e, s, jnp.max(s, axis=0, keepdims=True), j)

    def masked(s):
        keys = lax.broadcasted_iota(jnp.int32, s.shape, 0)
        queries = lax.broadcasted_iota(jnp.int32, s.shape, 1)
        return jnp.where(keys <= queries, s, NEG_BIG)

    for e in range(heads):
        scores(e, 0, 0)

    def body(t, carry):
        j = 2 * t
        for e in range(heads):
            scores(e, j + 1, 1)
            update_full(e, 0, j)
        for e in range(heads):
            scores(e, j + 2, 0)
            update_full(e, 1, j + 1)
        return carry

    lax.fori_loop(0, i // 2, body, 0)

    @pl.when(i % 2 == 0)
    def _():
        for e in range(heads):
            update_diagonal(e, 0, i)

    @pl.when(i % 2 == 1)
    def _():
        for e in range(heads):
            scores(e, i, 1)
            update_full(e, 0, i - 1)
        for e in range(heads):
            update_diagonal(e, 1, i)

    for e in range(heads):
        o_ref[0, :, V_HEAD_DIM * e:V_HEAD_DIM * (e + 1)] = (
            jnp.transpose(acc_sc[e] / l_sc[e]).astype(o_ref.dtype))


def _attention(q, k, vt):
    b, lp, _ = q.shape
    tile = SEQ_TILE
    hs = ATTN_STEP_HEADS
    return pl.pallas_call(
        functools.partial(_attn_kernel, tile=tile, heads=hs),
        grid=(b, ATTN_HEADS // hs, lp // tile),
        in_specs=[pl.BlockSpec((1, tile, hs * HEAD_PAD), lambda bi, hi, qi: (bi, qi, hi)),
                  pl.BlockSpec((1, lp, hs * HEAD_PAD), lambda bi, hi, qi: (bi, 0, hi)),
                  pl.BlockSpec((1, hs * V_HEAD_DIM, lp), lambda bi, hi, qi: (bi, hi, 0))],
        out_specs=pl.BlockSpec((1, tile, hs * V_HEAD_DIM), lambda bi, hi, qi: (bi, qi, hi)),
        out_shape=jax.ShapeDtypeStruct((b, lp, ATTN_WIDTH), BF16),
        scratch_shapes=[pltpu.VMEM((hs, 2, tile, tile), F32), pltpu.VMEM((hs, 2, 1, tile), F32),
                        pltpu.VMEM((hs, 1, tile), F32), pltpu.VMEM((hs, 1, tile), F32),
                        pltpu.VMEM((hs, V_HEAD_DIM, tile), F32)],
        compiler_params=pltpu.CompilerParams(
            dimension_semantics=("arbitrary", "arbitrary", "arbitrary"),
            vmem_limit_bytes=VMEM_LIMIT),
        name="attention",
    )(q, k, vt)


def _regroup(v):
    lane = lax.broadcasted_iota(jnp.int32, v[0].shape, 1)
    first_half = lane < 2 * PAIR
    even_block = (lane // PAIR) % 2 == 0
    swap = lambda x: pltpu.roll(x, 2 * PAIR, 1)
    t0 = jnp.where(first_half, v[0], swap(v[2]))
    t2 = jnp.where(first_half, swap(v[0]), v[2])
    t1 = jnp.where(first_half, v[1], swap(v[3]))
    t3 = jnp.where(first_half, swap(v[1]), v[3])
    up = lambda x: pltpu.roll(x, PAIR, 1)
    down = lambda x: pltpu.roll(x, LANE - PAIR, 1)
    return (jnp.where(even_block, t0, up(t1)), jnp.where(even_block, down(t0), t1),
            jnp.where(even_block, t2, up(t3)), jnp.where(even_block, down(t2), t3))


def _ssm_kernel(u_ref, ws_ref, a_ref, t_ref, wc_ref, d_ref, o_ref, x_sc, v_sc, sin_sc, st_sc,
                *, batch, nsub):
    nslab = 2 * PAIRS
    half = PAIRS

    @pl.when(pl.program_id(1) == 0)
    def _():
        st_sc[...] = jnp.zeros(st_sc.shape, F32)

    for b in range(batch):
        for m in range(SUB // PAIRS):
            pieces = _regroup([u_ref[b, pl.ds(PAIRS * m + i, nsub, stride=SUB), :] for i in range(PAIRS)])
            for j in range(PAIRS):
                x_sc[j, b * nsub:(b + 1) * nsub, LANE * m:LANE * (m + 1)] = pieces[j]
    for j in range(PAIRS):
        v = jnp.dot(x_sc[j].astype(BF16), ws_ref[0, j], preferred_element_type=F32)
        v_sc[j] = v[:, :LANE]
        v_sc[half + j] = v[:, LANE:]
    a_re = [jnp.broadcast_to(a_ref[0, 0:1, LANE * j:LANE * (j + 1)], (batch, LANE)) for j in range(half)]
    a_im = [jnp.broadcast_to(a_ref[0, 1:2, LANE * j:LANE * (j + 1)], (batch, LANE)) for j in range(half)]

    def body(r, carry):
        rows = pl.ds(r, batch, stride=nsub)
        out = []
        for j in range(half):
            s_re, s_im = carry[j], carry[half + j]
            sin_sc[j, rows, :] = s_re
            sin_sc[half + j, rows, :] = s_im
            out.append((a_re[j] * s_re - a_im[j] * s_im + v_sc[j, rows, :],
                        a_re[j] * s_im + a_im[j] * s_re + v_sc[half + j, rows, :]))
        return tuple(o[0] for o in out) + tuple(o[1] for o in out)

    state = lax.fori_loop(0, nsub, body, tuple(st_sc[j] for j in range(nslab)), unroll=4)
    for j in range(nslab):
        st_sc[j] = state[j]
    for n in range(X_W // MXU):
        cols = slice(MXU * n, MXU * (n + 1))
        kdim = MXU * (n + 1)
        g = []
        for j in range(PAIRS):
            s_in = jnp.concatenate([sin_sc[j], sin_sc[half + j]], axis=1).astype(BF16)
            y = (jnp.dot(x_sc[j, :, :kdim].astype(BF16), t_ref[0, j, :kdim, cols], preferred_element_type=F32)
                 + jnp.dot(s_in, wc_ref[0, j, :, cols], preferred_element_type=F32)
                 + d_ref[0, j, :, cols] * x_sc[j, :, cols])
            g.append(jax.nn.gelu(y))
        for mm in range(MXU // LANE):
            m = n * (MXU // LANE) + mm
            for b in range(batch):
                pieces = _regroup([gj[b * nsub:(b + 1) * nsub, LANE * mm:LANE * (mm + 1)] for gj in g])
                for i in range(PAIRS):
                    o_ref[b, pl.ds(PAIRS * m + i, nsub, stride=SUB), :] = pieces[i]


def _ssm(u, ws, a16, toep, wc, d_x, *, layer):
    batch, lp, _ = u.shape
    nsub = SSM_TILE_SUB
    rows = batch * nsub
    nslab = 2 * PAIRS
    wspec = lambda a: pl.BlockSpec((None, 1) + a.shape[2:], lambda q, c: (layer, q) + (0,) * (a.ndim - 2),
                                   pipeline_mode=pl.Buffered(1))
    return pl.pallas_call(
        functools.partial(_ssm_kernel, batch=batch, nsub=nsub),
        grid=(QUADS, lp // SSM_TILE),
        in_specs=[pl.BlockSpec((batch, SSM_TILE, LANE), lambda q, c: (0, c, q)),
                  wspec(ws), wspec(a16), wspec(toep), wspec(wc), wspec(d_x)],
        out_specs=pl.BlockSpec((batch, SSM_TILE, LANE), lambda q, c: (0, c, q)),
        out_shape=jax.ShapeDtypeStruct((batch, lp, SSM_WIDTH), F32),
        scratch_shapes=[pltpu.VMEM((PAIRS, rows, X_W), F32), pltpu.VMEM((nslab, rows, LANE), F32),
                        pltpu.VMEM((nslab, rows, LANE), F32), pltpu.VMEM((nslab, batch, LANE), F32)],
        compiler_params=pltpu.CompilerParams(dimension_semantics=("arbitrary", "arbitrary"),
                                             vmem_limit_bytes=VMEM_LIMIT),
        name="ssm",
    )(u, ws, a16, toep, wc, d_x)


def _ssm_tables(a_re, a_im, log_dt, b_re, b_im, c_re, c_im, d_skip):
    g, p, hc = SSM_GROUPS, SSM_STATE, SSM_GROUP
    npair = g // 2
    hi = lax.Precision.HIGHEST
    lr = jnp.minimum(a_re.astype(F32), -1e-4)
    li = a_im.astype(F32)
    dt = jnp.exp(log_dt.astype(F32))[:, None]
    mag = jnp.exp(lr * dt)
    lam_re = mag * jnp.cos(li * dt)
    lam_im = mag * jnp.sin(li * dt)
    nr, ni = lam_re - 1.0, lam_im
    den = lr * lr + li * li
    coef_re = (nr * lr + ni * li) / den
    coef_im = (ni * lr - nr * li) / den
    br = jnp.transpose(b_re.astype(F32), (0, 2, 1))
    bi = jnp.transpose(b_im.astype(F32), (0, 2, 1))
    bbar_re = coef_re[:, None, :] * br - coef_im[:, None, :] * bi
    bbar_im = coef_re[:, None, :] * bi + coef_im[:, None, :] * br
    cr, ci = c_re.astype(F32), c_im.astype(F32)
    kk = jnp.arange(SUB + 1, dtype=F32)[None, :, None]
    ph = (li * dt)[:, None, :] * kk
    pmag = jnp.exp((lr * dt)[:, None, :] * kk)
    pw_re = pmag * jnp.cos(ph)
    pw_im = pmag * jnp.sin(ph)

    iota = lambda n: jnp.arange(n)
    row_g = lambda n, per: (iota(n)[:, None] // per) % 2
    col_g = lambda n, per: (iota(n)[None, :] // per) % 2
    kw = SUB * hc
    spread = ((iota(kw)[:, None] // hc == iota(X_W)[None, :] // PAIR)
              & (iota(kw)[:, None] % hc == iota(X_W)[None, :] % hc)).astype(F32)

    cl_re = cr[:, None] * pw_re[:, :SUB, None, :] - ci[:, None] * pw_im[:, :SUB, None, :]
    cl_im = cr[:, None] * pw_im[:, :SUB, None, :] + ci[:, None] * pw_re[:, :SUB, None, :]
    kern = jnp.einsum('ghp,gkop->ghko', jnp.concatenate([bbar_re, -bbar_im], axis=-1),
                      jnp.concatenate([cl_re, cl_im], axis=-1), precision=hi).reshape(g * hc, kw)
    shift = ((iota(kw)[None, :, None] // hc + iota(SUB)[:, None, None] == iota(kw)[None, None, :] // hc)
             & (iota(kw)[None, :, None] % hc == iota(kw)[None, None, :] % hc)).astype(F32)
    toep = jnp.einsum('ra,sac->src', kern, shift, precision=hi)
    toep = jnp.transpose(toep.reshape(SUB, npair, PAIR, kw), (1, 0, 2, 3)).reshape(npair, X_W, kw)
    toep = jnp.einsum('qra,ac->qrc', toep, spread, precision=hi)
    toep = jnp.where(row_g(X_W, hc) == col_g(X_W, hc), toep, 0.0).astype(BF16)

    kr = (SUB - 1) - kk[:, :SUB]
    rph = (li * dt)[:, None, :] * kr
    rmag = jnp.exp((lr * dt)[:, None, :] * kr)
    rv_re = jnp.transpose(rmag * jnp.cos(rph), (1, 0, 2))
    rv_im = jnp.transpose(rmag * jnp.sin(rph), (1, 0, 2))

    def ws_tile(w):
        w = jnp.transpose(w.reshape(SUB, npair, PAIR, p), (1, 0, 2, 3)).reshape(npair, X_W, p)
        w = jnp.tile(w, (1, 1, 2))
        return jnp.where(row_g(X_W, hc) == col_g(ST_W, p), w, 0.0).astype(BF16)

    ws = jnp.concatenate(
        [ws_tile(rv_re[:, :, None, :] * bbar_re[None] - rv_im[:, :, None, :] * bbar_im[None]),
         ws_tile(rv_re[:, :, None, :] * bbar_im[None] + rv_im[:, :, None, :] * bbar_re[None])],
        axis=-1)

    rep_t = (iota(SUB)[:, None] == iota(kw)[None, :] // hc).astype(F32)
    rep_o = (iota(hc)[:, None] == iota(kw)[None, :] % hc).astype(F32)
    c_t = lambda c: jnp.einsum('gpo,oc->gpc', jnp.transpose(c, (0, 2, 1)), rep_o, precision=hi)
    l_t = lambda w: jnp.einsum('gpt,tc->gpc', jnp.transpose(w[:, 1:], (0, 2, 1)), rep_t, precision=hi)
    cr_t, ci_t, lr_t, li_t = c_t(cr), c_t(ci), l_t(pw_re), l_t(pw_im)

    def wc_tile(w):
        w = jnp.einsum('qra,ac->qrc', w.reshape(npair, ST_W, kw), spread, precision=hi)
        return jnp.where(row_g(ST_W, p) == col_g(X_W, hc), w, 0.0).astype(BF16)

    wc = jnp.concatenate([wc_tile(cr_t * lr_t - ci_t * li_t),
                          wc_tile(-(cr_t * li_t + ci_t * lr_t))], axis=1)
    quad = lambda w: w.reshape((QUADS, PAIRS) + w.shape[1:])
    a16 = jnp.stack([pw_re[:, SUB].reshape(QUADS, LANE * PAIRS), pw_im[:, SUB].reshape(QUADS, LANE * PAIRS)],
                    axis=1)
    d_x = jnp.tile(d_skip.astype(F32).reshape(npair, 1, PAIR), (1, 1, SUB))
    return quad(ws), a16, quad(toep), quad(wc), quad(d_x)


def _mix_ffn_kernel(*refs, first, final, tiles_per_seq, meta_rows, seq_rows):
    if first:
        (x_ref, meta_ref, attn_ref, g_ref, wglu_ref, ga_ref, gs_ref, wo_ref, gffn_ref,
         wg_ref, wu_ref, wd_ref, gfin_ref, o_ref, h_sc) = refs
        h = _first_rows(x_ref, meta_ref, h_sc, pl.program_id(0) % tiles_per_seq,
                        meta_rows=meta_rows, seq_rows=seq_rows)
    else:
        (h_ref, attn_ref, g_ref, wglu_ref, ga_ref, gs_ref, wo_ref, gffn_ref,
         wg_ref, wu_ref, wd_ref, gfin_ref, o_ref) = refs
        h = h_ref[...]
    g = g_ref[...]
    ssm = g * jax.nn.sigmoid(jnp.dot(g.astype(BF16), wglu_ref[...], preferred_element_type=F32))
    a_n = _rms(attn_ref[...].astype(F32), ga_ref[...]).astype(BF16)
    s_n = _rms(ssm, gs_ref[...]).astype(BF16)
    h1 = (h
          + jnp.dot(a_n, wo_ref[:ATTN_WIDTH, :], preferred_element_type=F32)
          + jnp.dot(s_n, wo_ref[ATTN_WIDTH:, :], preferred_element_type=F32))
    hn = _rms(h1, gffn_ref[...]).astype(BF16)
    gate = jnp.dot(hn, wg_ref[...], preferred_element_type=F32)
    up = jnp.dot(hn, wu_ref[...], preferred_element_type=F32)
    act = (jax.nn.silu(gate) * up).astype(BF16)
    o_ref[...] = h1
    o_ref[...] += jnp.dot(act, wd_ref[...], preferred_element_type=F32)
    if final:
        o_ref[...] = _rms(o_ref[...], gfin_ref[...])


def _mix_ffn(h, meta, attn, g, wglu, ga, gs, wo, gffn, wg, wu, wd, gfin, *, layer, final, lp, first, count):
    n = attn.shape[0]
    from_x = meta is not None
    assert not (from_x and final)
    row = lambda i: (i, 0)
    full = lambda a: _layer_spec(a, layer, pipeline_mode=pl.Buffered(1))
    if final:
        tm = OUT_TILE
        per_seq = count // tm
        n_out = (n // lp) * count
        align = math.gcd(lp, first, tm)
        src = lambda i: (pl.multiple_of((i // per_seq) * lp + first + (i % per_seq) * tm, align), 0)
        in_rows = lambda width: pl.BlockSpec((pl.Element(tm), pl.Element(width)), src)
    else:
        tm = SEQ_TILE
        n_out = n
        in_rows = lambda width: pl.BlockSpec((tm, width), row)
    if from_x:
        stream = [_x_window(tm, lp // tm, meta.shape[0], h.shape[1]), pl.BlockSpec(meta.shape, lambda i: (0, 0))]
        operands, scratch = (h, meta), [pltpu.VMEM((tm, D_MODEL), F32)]
    else:
        stream, operands, scratch = [in_rows(D_MODEL)], (h,), []
    return pl.pallas_call(
        functools.partial(_mix_ffn_kernel, first=from_x, final=final, tiles_per_seq=lp // tm,
                          meta_rows=first, seq_rows=count),
        grid=(n_out // tm,),
        scratch_shapes=scratch,
        in_specs=stream + [in_rows(ATTN_WIDTH), in_rows(SSM_WIDTH),
                  full(wglu), full(ga), full(gs), full(wo), full(gffn),
                  full(wg), full(wu), full(wd),
                  pl.BlockSpec(gfin.shape, lambda i: (0, 0), pipeline_mode=pl.Buffered(1))],
        out_specs=pl.BlockSpec((tm, D_MODEL), row),
        out_shape=jax.ShapeDtypeStruct((n_out, D_MODEL), F32),
        compiler_params=pltpu.CompilerParams(dimension_semantics=("arbitrary",),
                                             vmem_limit_bytes=VMEM_LIMIT),
        name="mix_ffn",
    )(*operands, attn, g, wglu, ga, gs, wo, gffn, wg, wu, wd, gfin)


def _prep_w_in(w):
    cq = w[:, :Q_LORA_RANK]
    ckv = w[:, Q_LORA_RANK:Q_LORA_RANK + KV_LORA_RANK]
    kr = w[:, Q_LORA_RANK + KV_LORA_RANK:Q_LORA_RANK + KV_LORA_RANK + QK_ROPE_DIM]
    u = w[:, Q_LORA_RANK + KV_LORA_RANK + QK_ROPE_DIM:]
    x1, x2 = kr[:, :QK_ROPE_DIM // 2], kr[:, QK_ROPE_DIM // 2:]
    return jnp.concatenate([cq, ckv, u, x1, x2, x2, x1], axis=1).astype(BF16)


def _prep_w_uq(w):
    w = w.reshape(Q_LORA_RANK, ATTN_HEADS, QK_NOPE_DIM + QK_ROPE_DIM)
    nope = w[..., :QK_NOPE_DIM]
    x1 = w[..., QK_NOPE_DIM:QK_NOPE_DIM + QK_ROPE_DIM // 2]
    x2 = w[..., QK_NOPE_DIM + QK_ROPE_DIM // 2:]
    return jnp.concatenate([nope, x1, x2, x2, x1], axis=-1).reshape(Q_LORA_RANK, ATTN_HEADS * HEAD_PAD).astype(BF16)


def _prep_w_ukv(w):
    w = w.reshape(KV_LORA_RANK, ATTN_HEADS, QK_NOPE_DIM + V_HEAD_DIM)
    k = w[..., :QK_NOPE_DIM].reshape(KV_LORA_RANK, ATTN_HEADS * QK_NOPE_DIM)
    v = w[..., QK_NOPE_DIM:].reshape(KV_LORA_RANK, ATTN_WIDTH)
    return k.astype(BF16), jnp.transpose(v).astype(BF16)


def kernel(x, meta_tokens, norm_mix_g, w_in, q_norm_g, w_uq, kv_norm_g, w_ukv, ssm_a_re, ssm_a_im, ssm_log_dt, ssm_b_re, ssm_b_im, ssm_c_re, ssm_c_im, ssm_d, w_glu, attn_out_g, ssm_out_g, w_o, norm_ffn_g, w_gate, w_up, w_down, final_norm_g):
    b, seq, d = x.shape
    m = NUM_META
    length = seq + m
    lp = -(-length // SEQ_TILE) * SEQ_TILE
    depth = w_in.shape[0]
    h, meta = x, meta_tokens.astype(x.dtype)
    pos_f = jnp.arange(lp, dtype=F32)
    inv_freq = 1.0 / (ROPE_THETA ** (jnp.arange(0, QK_ROPE_DIM, 2, dtype=F32) / QK_ROPE_DIM))
    ang = pos_f[:, None] * inv_freq[None, :]
    cos, sin = jnp.cos(ang), jnp.sin(ang)
    zpad = jnp.zeros((lp, LANE - QK_ROPE_DIM), F32)
    ctab = jnp.concatenate([cos, cos, zpad], axis=1)
    stab = jnp.concatenate([-sin, sin, zpad], axis=1)
    gain = lambda v: v.astype(F32)[:, None, :]
    gmix, gq, gkv, ga, gs, gffn = map(gain, (norm_mix_g, q_norm_g, kv_norm_g, attn_out_g, ssm_out_g, norm_ffn_g))
    win, wuq = jax.vmap(_prep_w_in)(w_in), jax.vmap(_prep_w_uq)(w_uq)
    wuk, wuvt = jax.vmap(_prep_w_ukv)(w_ukv)
    tables = jax.vmap(_ssm_tables)(ssm_a_re, ssm_a_im, ssm_log_dt, ssm_b_re, ssm_b_im, ssm_c_re, ssm_c_im, ssm_d)
    wglu, wo, wg, wu, wd = (w.astype(BF16) for w in (w_glu, w_o, w_gate, w_up, w_down))
    gfin = final_norm_g.reshape(1, -1).astype(F32)
    for l in range(depth):
        q, k, vt, u = _proj(h, meta, ctab, stab, gmix, win, gq, wuq, gkv, wuk, wuvt,
                            layer=l, tiles_per_seq=lp // SEQ_TILE, n=b * lp)
        attn = _attention(q.reshape(b, lp, -1), k.reshape(b, lp, -1), vt)
        g = _ssm(u.reshape(b, lp, SSM_WIDTH), *tables, layer=l)
        h = _mix_ffn(h, meta, attn.reshape(b * lp, -1), g.reshape(b * lp, SSM_WIDTH),
                     wglu, ga, gs, wo, gffn, wg, wu, wd, gfin,
                     layer=l, final=(l == depth - 1), lp=lp, first=m, count=seq)
        meta = None
    return h.reshape(b, seq, d)
```
